```python
import jax, jax.numpy as jnp
from jax import lax
import numpy as np

D_MODEL = 1024
BATCH = 16
SEQ = 4096
DEPTH = 1
DEC_BATCH = 128
DEC_SEQ = 1
PAST_LEN = 8192
PAGE_SIZE = 128

MIX_WIDTH = D_MODEL
ATT_HEADS = 8
HEAD_DIM = 64
ATT_WIDTH = ATT_HEADS * HEAD_DIM
GM_GROUPS = 4
GM_WIDTH = MIX_WIDTH - ATT_WIDTH
GM_GROUP_DIM = GM_WIDTH // GM_GROUPS
GM_CHUNK = 128
IN_WIDTH = 3 * ATT_WIDTH + 2 * GM_WIDTH
MOBA_BLOCK = 256
MOBA_TOP_K = 3
MOBA_QUERY_ROWS = 128
N_EXPERTS = 64
EXPERT_DIM = 256
SHARED_DIM = 256
MOE_TOP_K = 8
N_EXPERT_GROUPS = 8
TOPK_GROUPS = 4
ROUTED_SCALE = 2.5
MOE_TOKEN_CHUNK = 1024
DEEPNORM_ALPHA = (2 * DEPTH) ** 0.25
DEEPNORM_BETA = (8 * DEPTH) ** -0.25
LN_EPS = 1e-5
NEG = -1e30

kernel_name = "hymba_moba_gmlp_moe_step"


def layer_norm(x, g, b):
    xf = x.astype(jnp.float32)
    mu = jnp.mean(xf, axis=-1, keepdims=True)
    var = jnp.mean(jnp.square(xf - mu), axis=-1, keepdims=True)
    y = (xf - mu) * lax.rsqrt(var + LN_EPS) * g.astype(jnp.float32) + b.astype(jnp.float32)
    return y.astype(x.dtype)


def split_heads(a):
    n, t, _ = a.shape
    return a.reshape(n, t, ATT_HEADS, HEAD_DIM).transpose(0, 2, 1, 3)


def merge_heads(a):
    n, h, t, d = a.shape
    return a.transpose(0, 2, 1, 3).reshape(n, t, h * d)


def input_projection(x, w_in):
    z = jnp.einsum("ntd,dc->ntc", x, w_in)
    q, k, v, gu, gv = jnp.split(z, [ATT_WIDTH, 2 * ATT_WIDTH, 3 * ATT_WIDTH, 3 * ATT_WIDTH + GM_WIDTH], axis=-1)
    return split_heads(q) * (HEAD_DIM ** -0.5), split_heads(k), split_heads(v), gu, gv


def moba_chunk(q, q_pos, k_mean, fetch):
    n, h, qc, _ = q.shape
    nb = k_mean.shape[2]
    own = q_pos // MOBA_BLOCK
    gate = jnp.einsum("nhqd,nhbd->nhqb", q, k_mean, preferred_element_type=jnp.float32)
    past = jnp.arange(nb, dtype=jnp.int32)[None, :] < own[:, None]
    gate = jnp.where(past, gate, NEG)
    _, top_i = lax.top_k(gate, min(MOBA_TOP_K, nb))
    top_i = top_i.astype(jnp.int32)
    own_b = jnp.broadcast_to(own[:, None], (n, h, qc, 1))
    idx = jnp.concatenate([top_i, own_b], axis=-1)
    valid = jnp.concatenate([top_i < own[:, None], jnp.ones(own_b.shape, bool)], axis=-1)
    k_sel, v_sel = fetch(idx)
    key_pos = idx[..., None] * MOBA_BLOCK + jnp.arange(MOBA_BLOCK, dtype=jnp.int32)
    mask = valid[..., None] & (key_pos <= q_pos[:, None, None])
    logits = jnp.einsum("nhqd,nhqjtd->nhqjt", q, k_sel, preferred_element_type=jnp.float32)
    logits = jnp.where(mask, logits, NEG).reshape(n, h, qc, -1)
    p = jax.nn.softmax(logits, axis=-1).reshape(mask.shape).astype(v_sel.dtype)
    return jnp.einsum("nhqjt,nhqjtd->nhqd", p, v_sel)


def moba_attend(q, q_pos, k_mean, fetch):
    n, h, t, d = q.shape
    qc = max(1, min(t, MOBA_QUERY_ROWS // n))
    nc = -(-t // qc)
    pad = nc * qc - t
    qp = jnp.pad(q, ((0, 0), (0, 0), (0, pad), (0, 0)))
    pos = jnp.pad(q_pos, (0, pad), mode="edge")
    q_blocks = qp.reshape(n, h, nc, qc, d).transpose(2, 0, 1, 3, 4)
    out = lax.map(lambda a: moba_chunk(a[0], a[1], k_mean, fetch), (q_blocks, pos.reshape(nc, qc)))
    return out.transpose(1, 2, 0, 3, 4).reshape(n, h, nc * qc, d)[:, :, :t]


def moba_prompt(q, k, v):
    n, h, t, d = k.shape
    nb = -(-t // MOBA_BLOCK)
    pad = nb * MOBA_BLOCK - t
    kb = jnp.pad(k, ((0, 0), (0, 0), (0, pad), (0, 0))).reshape(n, h, nb, MOBA_BLOCK, d)
    vb = jnp.pad(v, ((0, 0), (0, 0), (0, pad), (0, 0))).reshape(n, h, nb, MOBA_BLOCK, d)
    k_mean = jnp.sum(kb, axis=3, dtype=jnp.float32) / MOBA_BLOCK
    n_i = jnp.arange(n)[:, None, None, None]
    h_i = jnp.arange(h)[None, :, None, None]

    def fetch(idx):
        return kb[n_i, h_i, idx], vb[n_i, h_i, idx]

    return moba_attend(q, jnp.arange(t, dtype=jnp.int32), k_mean, fetch)


def moba_sample(q, k, v, cache_k, cache_v, page_table):
    n, h, t, d = q.shape
    page = cache_k.shape[2]
    n_pages = page_table.shape[1]
    past_len = n_pages * page
    ppb = MOBA_BLOCK // page
    npn = -(-t // page)

    def to_pages(a):
        a = jnp.pad(a, ((0, 0), (0, 0), (0, npn * page - t), (0, 0)))
        return a.reshape(n, h, npn, page, d)

    kn, vn = to_pages(k), to_pages(v)
    past_sum = jnp.sum(cache_k, axis=2, dtype=jnp.float32)[page_table]
    new_sum = jnp.sum(kn, axis=3, dtype=jnp.float32).transpose(0, 2, 1, 3)
    page_sum = jnp.concatenate([past_sum, new_sum], axis=1)
    nb = -(-(n_pages + npn) // ppb)
    page_sum = jnp.pad(page_sum, ((0, 0), (0, nb * ppb - n_pages - npn), (0, 0), (0, 0)))
    k_mean = page_sum.reshape(n, nb, ppb, h, d).sum(axis=2).transpose(0, 2, 1, 3) / MOBA_BLOCK
    n_i = jnp.arange(n)[:, None, None, None, None]
    h_i = jnp.arange(h)[None, :, None, None, None]
    offs = jnp.arange(ppb, dtype=jnp.int32)

    def fetch(idx):
        pages = idx[..., None] * ppb + offs
        in_cache = (pages < n_pages)[..., None, None]
        phys = page_table[n_i, jnp.clip(pages, 0, n_pages - 1)]
        new_i = jnp.clip(pages - n_pages, 0, npn - 1)

        def take(pool, new):
            rows = jnp.where(in_cache, pool[phys, h_i], new[n_i, h_i, new_i])
            return rows.reshape(rows.shape[:4] + (MOBA_BLOCK, d))

        return take(cache_k, kn), take(cache_v, vn)

    q_pos = past_len + jnp.arange(t, dtype=jnp.int32)
    return moba_attend(q, q_pos, k_mean, fetch)


def chunk_gmlp(gu, gv, ln_g, ln_b, w_s, b_s):
    n, t, _ = gu.shape
    u = jax.nn.gelu(gu)
    vg = jax.nn.gelu(gv).reshape(n, t, GM_GROUPS, GM_GROUP_DIM)
    vn = layer_norm(vg, ln_g.reshape(GM_GROUPS, GM_GROUP_DIM), ln_b.reshape(GM_GROUPS, GM_GROUP_DIM))
    nck = -(-t // GM_CHUNK)
    pad = nck * GM_CHUNK - t
    vc = jnp.pad(vn, ((0, 0), (0, pad), (0, 0), (0, 0))).reshape(n, nck, GM_CHUNK, GM_GROUPS, GM_GROUP_DIM)
    causal = jnp.tril(jnp.ones((GM_CHUNK, GM_CHUNK), bool))
    w = jnp.where(causal, w_s, 0)
    s = jnp.einsum("gij,ncjgd->ncigd", w, vc) + b_s.T[None, None, :, :, None]
    s = s.reshape(n, nck * GM_CHUNK, GM_WIDTH)[:, :t]
    return u * s, vn.reshape(n, t, GM_WIDTH)


def moe_ffn(h, w_router, router_bias, w_gate, w_up, w_down, ws_gate, ws_up, ws_down):
    n, t, dm = h.shape
    x = h.reshape(n * t, dm)
    m = x.shape[0]
    scores = jax.nn.sigmoid(jnp.einsum("md,de->me", x, w_router, preferred_element_type=jnp.float32))
    biased = scores + router_bias.astype(jnp.float32)
    per_group = N_EXPERTS // N_EXPERT_GROUPS
    group_score = lax.top_k(biased.reshape(m, N_EXPERT_GROUPS, per_group), 2)[0].sum(-1)
    _, top_g = lax.top_k(group_score, TOPK_GROUPS)
    rows = jnp.arange(m)[:, None]
    group_ok = jnp.zeros((m, N_EXPERT_GROUPS), bool).at[rows, top_g].set(True)
    expert_ok = jnp.repeat(group_ok, per_group, axis=1)
    _, top_e = lax.top_k(jnp.where(expert_ok, biased, NEG), MOE_TOP_K)
    w = jnp.take_along_axis(scores, top_e, axis=-1)
    w = w / jnp.sum(w, axis=-1, keepdims=True) * ROUTED_SCALE
    gates = jnp.zeros((m, N_EXPERTS), jnp.float32).at[rows, top_e].set(w).astype(x.dtype)
    ct = min(MOE_TOKEN_CHUNK, m)
    nc = -(-m // ct)
    pad = nc * ct - m
    xs = jnp.pad(x, ((0, pad), (0, 0))).reshape(nc, ct, dm)
    gs = jnp.pad(gates, ((0, pad), (0, 0))).reshape(nc, ct, N_EXPERTS)

    def experts(a):
        xc, gc = a
        hg = jnp.einsum("td,edf->tef", xc, w_gate)
        hu = jnp.einsum("td,edf->tef", xc, w_up)
        return jnp.einsum("tef,efd->td", jax.nn.silu(hg) * hu * gc[..., None], w_down)

    routed = lax.map(experts, (xs, gs)).reshape(nc * ct, dm)[:m]
    shared = jnp.einsum("mf,fd->md", jax.nn.silu(x @ ws_gate) * (x @ ws_up), ws_down)
    return (routed + shared).reshape(n, t, dm)


def block_output(x, att, gm, w_o, ln1_g, ln1_b, ln2_g, ln2_b,
                 w_router, router_bias, w_gate, w_up, w_down, ws_gate, ws_up, ws_down):
    mix = jnp.einsum("ntc,cd->ntd", jnp.concatenate([merge_heads(att), gm], axis=-1), w_o)
    h = layer_norm(DEEPNORM_ALPHA * x + mix, ln1_g, ln1_b)
    ffn = moe_ffn(h, w_router, router_bias, w_gate, w_up, w_down, ws_gate, ws_up, ws_down)
    return layer_norm(DEEPNORM_ALPHA * h + ffn, ln2_g, ln2_b)


def setup_inputs(seed: int = 0) -> dict:
    key = jax.random.key(seed)
    ks = jax.random.split(key, 24)
    f32 = jnp.float32
    n_pages = PAST_LEN // PAGE_SIZE
    n_used = DEC_BATCH * n_pages
    n_pool = n_used + -(-n_used // 4)

    def nrm(k, shape, scale):
        return jax.random.normal(k, shape, f32) * scale

    page_table = jax.random.permutation(ks[4], n_pool)[:n_used].reshape(DEC_BATCH, n_pages).astype(jnp.int32)
    return {
        "x_prompt": nrm(ks[0], (BATCH, SEQ, D_MODEL), 1.0),
        "x_sample": nrm(ks[1], (DEC_BATCH, DEC_SEQ, D_MODEL), 1.0),
        "cache_k": nrm(ks[2], (DEPTH, n_pool, ATT_HEADS, PAGE_SIZE, HEAD_DIM), 1.0),
        "cache_v": nrm(ks[3], (DEPTH, n_pool, ATT_HEADS, PAGE_SIZE, HEAD_DIM), 1.0),
        "page_table": page_table,
        "w_in": nrm(ks[5], (DEPTH, D_MODEL, IN_WIDTH), D_MODEL ** -0.5),
        "w_o": nrm(ks[6], (DEPTH, MIX_WIDTH, D_MODEL), MIX_WIDTH ** -0.5 * DEEPNORM_BETA),
        "gm_ln_g": 1.0 + nrm(ks[7], (DEPTH, GM_WIDTH), 0.02),
        "gm_ln_b": nrm(ks[8], (DEPTH, GM_WIDTH), 0.02),
        "gm_w_s": nrm(ks[9], (DEPTH, GM_GROUPS, GM_CHUNK, GM_CHUNK), 0.5 * GM_CHUNK ** -0.5),
        "gm_b_s": 1.0 + nrm(ks[10], (DEPTH, GM_GROUPS, GM_CHUNK), 0.1),
        "ln1_g": 1.0 + nrm(ks[11], (DEPTH, D_MODEL), 0.02),
        "ln1_b": nrm(ks[12], (DEPTH, D_MODEL), 0.02),
        "w_router": nrm(ks[13], (DEPTH, D_MODEL, N_EXPERTS), D_MODEL ** -0.5),
        "router_bias": nrm(ks[14], (DEPTH, N_EXPERTS), 0.01),
        "w_gate": nrm(ks[15], (DEPTH, N_EXPERTS, D_MODEL, EXPERT_DIM), D_MODEL ** -0.5),
        "w_up": nrm(ks[16], (DEPTH, N_EXPERTS, D_MODEL, EXPERT_DIM), D_MODEL ** -0.5),
        "w_down": nrm(ks[17], (DEPTH, N_EXPERTS, EXPERT_DIM, D_MODEL), EXPERT_DIM ** -0.5 * DEEPNORM_BETA),
        "ws_gate": nrm(ks[18], (DEPTH, D_MODEL, SHARED_DIM), D_MODEL ** -0.5),
        "ws_up": nrm(ks[19], (DEPTH, D_MODEL, SHARED_DIM), D_MODEL ** -0.5),
        "ws_down": nrm(ks[20], (DEPTH, SHARED_DIM, D_MODEL), SHARED_DIM ** -0.5 * DEEPNORM_BETA),
        "ln2_g": 1.0 + nrm(ks[21], (DEPTH, D_MODEL), 0.02),
        "ln2_b": nrm(ks[22], (DEPTH, D_MODEL), 0.02),
    }


def reference(x_prompt, x_sample, cache_k, cache_v, page_table, w_in, w_o, gm_ln_g, gm_ln_b, gm_w_s, gm_b_s,
              ln1_g, ln1_b, w_router, router_bias, w_gate, w_up, w_down, ws_gate, ws_up, ws_down, ln2_g, ln2_b):
    xp, xs = x_prompt, x_sample
    k_prompt, v_prompt, k_sample, v_sample, gm_v_sample = [], [], [], [], []
    for l in range(DEPTH):
        out_w = (w_o[l], ln1_g[l], ln1_b[l], ln2_g[l], ln2_b[l], w_router[l], router_bias[l],
                 w_gate[l], w_up[l], w_down[l], ws_gate[l], ws_up[l], ws_down[l])
        gm_w = (gm_ln_g[l], gm_ln_b[l], gm_w_s[l], gm_b_s[l])
        q, k, v, gu, gv = input_projection(xp, w_in[l])
        att = moba_prompt(q, k, v)
        gm, _ = chunk_gmlp(gu, gv, *gm_w)
        xp = block_output(xp, att, gm, *out_w)
        n, h, t, d = k.shape
        k_prompt.append(k.reshape(n, h, t // PAGE_SIZE, PAGE_SIZE, d).transpose(0, 2, 1, 3, 4))
        v_prompt.append(v.reshape(n, h, t // PAGE_SIZE, PAGE_SIZE, d).transpose(0, 2, 1, 3, 4))
        q, k, v, gu, gv = input_projection(xs, w_in[l])
        att = moba_sample(q, k, v, cache_k[l], cache_v[l], page_table)
        gm, gv_rows = chunk_gmlp(gu, gv, *gm_w)
        xs = block_output(xs, att, gm, *out_w)
        k_sample.append(k)
        v_sample.append(v)
        gm_v_sample.append(gv_rows)
    return (xp, xs, jnp.stack(k_prompt), jnp.stack(v_prompt), jnp.stack(k_sample), jnp.stack(v_sample), jnp.stack(gm_v_sample))
```

```python
import functools

import jax
import jax.numpy as jnp
from jax import lax
from jax.experimental import pallas as pl
from jax.experimental.pallas import tpu as pltpu

F32 = jnp.float32
BF16 = jnp.bfloat16

MOBA_BLOCK = 256
BLOCK_SHIFT = 8
MOBA_TOP_K = 3
MOE_TOP_K = 8
N_EXPERT_GROUPS = 8
TOPK_GROUPS = 4
ROUTED_SCALE = 2.5
LN_EPS = 1e-5
NEG = -1e30
MASK_BIG = float(2.0 ** 100)
LANES = 128
V7X_VMEM_BYTES = 64 * 1024 * 1024


def _cparams(semantics, vmem_mb):
    assert vmem_mb * 1024 * 1024 < V7X_VMEM_BYTES
    return pltpu.CompilerParams(dimension_semantics=semantics, vmem_limit_bytes=vmem_mb * 1024 * 1024)


def _dot(a, b):
    return jnp.dot(a, b, preferred_element_type=F32)


def _dot_nt(a, b):
    return lax.dot_general(a, b, (((1,), (1,)), ((), ())), preferred_element_type=F32)


def _layer_norm_rows(y, g, b):
    mu = jnp.mean(y, axis=-1, keepdims=True)
    var = jnp.mean(jnp.square(y - mu), axis=-1, keepdims=True)
    return (y - mu) * lax.rsqrt(var + LN_EPS) * g + b


def _sigmoid(x):
    return 1.0 / (1.0 + jnp.exp(-x))


def _gmlp_norm(zgv, lng, lnb, groups, gdim):
    vg = jax.nn.gelu(zgv)
    out = []
    for g in range(groups):
        sl = slice(g * gdim, (g + 1) * gdim)
        out.append(_layer_norm_rows(vg[:, sl], lng[:, sl], lnb[:, sl]))
    return out


def _proj_prompt_kernel(x_ref, w_ref, lng_ref, lnb_ref, ws_ref, bias_ref,
                        q_ref, k_ref, v_ref, ko_ref, vo_ref, gm_ref, *, heads, hd, groups, chunk):
    tm = x_ref.shape[1]
    aw = heads * hd
    gdim = gm_ref.shape[2] // groups
    x = x_ref[0].astype(BF16)
    zq = _dot(x, w_ref[:, 0:aw])
    zk = _dot(x, w_ref[:, aw:2 * aw])
    zv = _dot(x, w_ref[:, 2 * aw:3 * aw])

    lane = lax.broadcasted_iota(jnp.int32, (tm, LANES), 1)
    row = lax.broadcasted_iota(jnp.int32, (tm, LANES), 0)
    blk = (pl.program_id(1) * tm + row) >> BLOCK_SHIFT
    lo = lane < hd
    for p in range(heads // 2):
        sl = slice(p * LANES, (p + 1) * LANES)
        cq = zq[:, sl] * (hd ** -0.5)
        ck = zk[:, sl]
        cv = zv[:, sl]
        q_ref[0, 2 * p] = jnp.where(lo, cq, 0.0).astype(BF16)
        q_ref[0, 2 * p + 1] = jnp.where(lo, 0.0, cq).astype(BF16)
        k_ref[0, 2 * p] = jnp.where(lo, ck, jnp.where(lane - hd == blk, -MASK_BIG, 0.0)).astype(BF16)
        k_ref[0, 2 * p + 1] = jnp.where(lo, jnp.where(lane == blk, -MASK_BIG, 0.0), ck).astype(BF16)
        v_ref[0, 2 * p] = jnp.where(lo, cv, jnp.where(lane == hd, 1.0, 0.0)).astype(BF16)
        v_ref[0, 2 * p + 1] = jnp.where(lo, jnp.where(lane == 0, 1.0, 0.0), cv).astype(BF16)

    page = ko_ref.shape[3]
    for c in range(tm // page):
        for h in range(heads):
            ko_ref[0, c, h] = zk[c * page:(c + 1) * page, h * hd:(h + 1) * hd]
            vo_ref[0, c, h] = zv[c * page:(c + 1) * page, h * hd:(h + 1) * hd]

    gw = groups * gdim
    u = jax.nn.gelu(_dot(x, w_ref[:, 3 * aw:3 * aw + gw]))
    vn = _gmlp_norm(_dot(x, w_ref[:, 3 * aw + gw:3 * aw + 2 * gw]), lng_ref[...], lnb_ref[...], groups, gdim)
    ci = lax.broadcasted_iota(jnp.int32, (chunk, chunk), 0)
    cj = lax.broadcasted_iota(jnp.int32, (chunk, chunk), 1)
    for g in range(groups):
        wm = jnp.where(ci >= cj, ws_ref[g], 0.0).astype(BF16)
        gs = slice(g * gdim, (g + 1) * gdim)
        for c in range(tm // chunk):
            rs = slice(c * chunk, (c + 1) * chunk)
            s = _dot(wm, vn[g][rs].astype(BF16)) + bias_ref[:, gs]
            gm_ref[0, rs, gs] = (u[rs, gs] * s).astype(BF16)


def _proj_sample_kernel(x_ref, w_ref, lng_ref, lnb_ref, w00_ref, b0_ref,
                        q_ref, k_ref, v_ref, gm_ref, vn_ref, *, heads, hd, groups):
    aw = heads * hd
    gw = vn_ref.shape[1]
    gdim = gw // groups
    x = x_ref[...].astype(BF16)
    q_ref[...] = _dot(x, w_ref[:, 0:aw]) * (hd ** -0.5)
    k_ref[...] = _dot(x, w_ref[:, aw:2 * aw])
    v_ref[...] = _dot(x, w_ref[:, 2 * aw:3 * aw])
    u = jax.nn.gelu(_dot(x, w_ref[:, 3 * aw:3 * aw + gw]))
    vn = _gmlp_norm(_dot(x, w_ref[:, 3 * aw + gw:3 * aw + 2 * gw]), lng_ref[...], lnb_ref[...], groups, gdim)
    for g in range(groups):
        gs = slice(g * gdim, (g + 1) * gdim)
        vn_ref[:, gs] = vn[g]
        gm_ref[:, gs] = (u[:, gs] * (w00_ref[:, gs] * vn[g] + b0_ref[:, gs])).astype(BF16)


def _proj_prompt(x, w_in, lng, lnb, w_s, b_s, heads, hd, page):
    n, t, dm = x.shape
    groups, chunk = w_s.shape[0], w_s.shape[1]
    gw = lng.shape[-1]
    gdim = gw // groups
    tm = 512
    assert t % tm == 0 and tm % MOBA_BLOCK == 0 and tm % page == 0 and tm % chunk == 0
    assert heads % 2 == 0 and 2 * hd == LANES and t // MOBA_BLOCK <= hd
    bias = jnp.repeat(b_s.T, gdim, axis=1)
    att_shape = jax.ShapeDtypeStruct((n, heads, t, LANES), BF16)
    page_shape = jax.ShapeDtypeStruct((n, t // page, heads, page, hd), F32)
    att_spec = pl.BlockSpec((1, heads, tm, LANES), lambda b, i: (b, 0, i, 0))
    page_spec = pl.BlockSpec((1, tm // page, heads, page, hd), lambda b, i: (b, i, 0, 0, 0))
    full = lambda a: pl.BlockSpec(a.shape, lambda b, i: (0,) * a.ndim)
    lng2, lnb2 = lng.reshape(1, gw), lnb.reshape(1, gw)
    return pl.pallas_call(
        functools.partial(_proj_prompt_kernel, heads=heads, hd=hd, groups=groups, chunk=chunk),
        grid=(n, t // tm),
        in_specs=[pl.BlockSpec((1, tm, dm), lambda b, i: (b, i, 0)), full(w_in), full(lng2), full(lnb2),
                  full(w_s), full(bias)],
        out_specs=[att_spec, att_spec, att_spec, page_spec, page_spec,
                   pl.BlockSpec((1, tm, gw), lambda b, i: (b, i, 0))],
        out_shape=[att_shape, att_shape, att_shape, page_shape, page_shape,
                   jax.ShapeDtypeStruct((n, t, gw), BF16)],
        compiler_params=_cparams(("parallel", "parallel"), 48),
        name="proj_prompt",
    )(x, w_in, lng2, lnb2, w_s, bias)


def _proj_sample(x, w_in, lng, lnb, w_s, b_s, heads, hd):
    m, dm = x.shape
    groups = w_s.shape[0]
    gw = lng.shape[-1]
    gdim = gw // groups
    aw = heads * hd
    w00 = jnp.repeat(w_s[:, 0, 0], gdim).reshape(1, gw)
    b0 = jnp.repeat(b_s[:, 0], gdim).reshape(1, gw)
    f = lambda c, dt: jax.ShapeDtypeStruct((m, c), dt)
    return pl.pallas_call(
        functools.partial(_proj_sample_kernel, heads=heads, hd=hd, groups=groups),
        out_shape=[f(aw, F32), f(aw, F32), f(aw, F32), f(gw, BF16), f(gw, F32)],
        compiler_params=_cparams(None, 32),
        name="proj_sample",
    )(x, w_in, lng.reshape(1, gw), lnb.reshape(1, gw), w00, b0)


def _moba_kernel(q_ref, k_ref, v_ref, o_ref, kmh_ref, kml_ref, m_ref, acc_ref, *, hd):
    t = q_ref.shape[2]
    bq = MOBA_BLOCK
    nb = t // bq
    lane = lax.broadcasted_iota(jnp.int32, (bq, LANES), 1)
    aux0 = (hd, 0)

    key_blk = lax.broadcasted_iota(jnp.int32, (LANES, t), 1) >> BLOCK_SHIFT
    out_row = lax.broadcasted_iota(jnp.int32, (LANES, t), 0)
    lane_sq = lax.broadcasted_iota(jnp.int32, (LANES, LANES), 1)
    for hh in range(2):
        avg = jnp.where(out_row - aux0[hh] == key_blk, 1.0 / bq, 0.0).astype(BF16)
        km_rows = _dot(avg, k_ref[0, hh])
        km_rows = jnp.where(lane_sq < hd if hh == 0 else lane_sq >= hd, km_rows, 0.0)
        hi = km_rows.astype(BF16)
        kmh_ref[hh] = hi
        kml_ref[hh] = (km_rows - hi.astype(F32)).astype(BF16)

    rq = lax.broadcasted_iota(jnp.int32, (bq, bq), 0)
    cq = lax.broadcasted_iota(jnp.int32, (bq, bq), 1)

    def q_block(i, carry):
        r0 = pl.multiple_of(i * bq, bq)
        q2 = []
        for hh in range(2):
            q = q_ref[0, hh, pl.ds(r0, bq), :]
            gate = _dot_nt(q, kmh_ref[hh]) + _dot_nt(q, kml_ref[hh])
            bidx = lane - aux0[hh]
            past = (bidx >= 0) & (bidx < i)
            g = jnp.where(past, gate, NEG)
            sel = jnp.zeros((bq, LANES), jnp.bool_)
            for _ in range(MOBA_TOP_K):
                mx = jnp.max(g, axis=-1, keepdims=True)
                first = jnp.min(jnp.where((g == mx) & past, lane, 2 * LANES), axis=-1, keepdims=True)
                pick = (lane == first) & (mx > 0.5 * NEG)
                sel = sel | pick
                g = jnp.where(pick, NEG, g)
            notsel = jnp.where(past & jnp.logical_not(sel), 1.0, 0.0)
            q2.append(q + notsel.astype(BF16))
            s = _dot_nt(q2[hh], k_ref[0, hh, pl.ds(r0, bq), :])
            s = jnp.where(cq <= rq, s, NEG)
            mx = jnp.max(s, axis=-1, keepdims=True)
            p = jnp.exp(s - mx)
            m_ref[hh] = jnp.broadcast_to(mx, (bq, LANES))
            acc_ref[hh] = _dot(p.astype(BF16), v_ref[0, hh, pl.ds(r0, bq), :])

        def kv_block(j, c):
            c0 = pl.multiple_of(j * bq, bq)
            for hh in range(2):
                s = _dot_nt(q2[hh], k_ref[0, hh, pl.ds(c0, bq), :])
                m_prev = m_ref[hh]
                m_new = jnp.maximum(m_prev, jnp.max(s, axis=-1, keepdims=True))
                alpha = jnp.exp(m_prev - m_new)
                p = jnp.exp(s - jnp.concatenate([m_new, m_new], axis=1))
                m_ref[hh] = m_new
                acc_ref[hh] = alpha * acc_ref[hh] + _dot(p.astype(BF16), v_ref[0, hh, pl.ds(c0, bq), :])
            return c

        lax.fori_loop(0, i, kv_block, 0)
        a0 = acc_ref[0]
        a1 = acc_ref[1]
        o0 = a0 / a0[:, hd:hd + 1]
        o1 = a1 / a1[:, 0:1]
        o_ref[0, pl.ds(r0, bq), :] = jnp.where(lane < hd, o0, o1).astype(o_ref.dtype)
        return carry

    lax.fori_loop(0, nb, q_block, 0)


def _moba_prompt(qp, kp, vp, hd):
    n, heads, t, _ = qp.shape
    spec = pl.BlockSpec((1, 2, t, LANES), lambda b, p: (b, p, 0, 0))
    return pl.pallas_call(
        functools.partial(_moba_kernel, hd=hd),
        grid=(n, heads // 2),
        in_specs=[spec, spec, spec],
        out_specs=pl.BlockSpec((1, t, LANES), lambda b, p: (b, 0, p)),
        out_shape=jax.ShapeDtypeStruct((n, t, heads * hd), BF16),
        scratch_shapes=[pltpu.VMEM((2, LANES, LANES), BF16), pltpu.VMEM((2, LANES, LANES), BF16),
                        pltpu.VMEM((2, MOBA_BLOCK, LANES), F32), pltpu.VMEM((2, MOBA_BLOCK, LANES), F32)],
        compiler_params=_cparams(("parallel", "parallel"), 40),
        name="moba_prompt",
    )(qp, kp, vp)


SGATE_PAGES = 16


def _sgate_kernel(pt_ref, q_ref, *refs, ppb):
    pages = refs[:SGATE_PAGES]
    idx_ref = refs[SGATE_PAGES]
    gate_ref = refs[SGATE_PAGES + 1]
    s = pl.program_id(1)
    bps = SGATE_PAGES // ppb
    q = q_ref[0]
    rows = []
    for b in range(bps):
        acc = jnp.sum(pages[b * ppb][...], axis=1)
        for c in range(1, ppb):
            acc = acc + jnp.sum(pages[b * ppb + c][...], axis=1)
        km = acc / MOBA_BLOCK
        rows.append(jnp.sum(km * q, axis=-1, keepdims=True))
    gate_ref[s] = jnp.concatenate(rows, axis=1)

    @pl.when(s == pl.num_programs(1) - 1)
    def _():
        nsteps = gate_ref.shape[0]
        g = jnp.concatenate([gate_ref[i] for i in range(nsteps)], axis=1)
        col = lax.broadcasted_iota(jnp.int32, g.shape, 1)
        picks = []
        for _ in range(MOBA_TOP_K):
            mx = jnp.max(g, axis=-1, keepdims=True)
            first = jnp.min(jnp.where(g == mx, col, g.shape[1]), axis=-1, keepdims=True)
            picks.append(first)
            g = jnp.where(col == first, NEG, g)
        picks.append(jnp.zeros((g.shape[0], LANES - MOBA_TOP_K), jnp.int32))
        idx_ref[0] = jnp.concatenate(picks, axis=1)


def _sample_gate(q3, cache_k, page_table):
    n, heads, hd = q3.shape
    page = cache_k.shape[2]
    n_pages = page_table.shape[1]
    ppb = MOBA_BLOCK // page
    assert n_pages % SGATE_PAGES == 0 and SGATE_PAGES % ppb == 0 and n_pages // ppb >= MOBA_TOP_K
    steps = n_pages // SGATE_PAGES

    def page_spec(c):
        return pl.BlockSpec((None, heads, page, hd),
                            lambda b, s, pt: (pt[b * n_pages + s * SGATE_PAGES + c], 0, 0, 0))

    grid_spec = pltpu.PrefetchScalarGridSpec(
        num_scalar_prefetch=1,
        grid=(n, steps),
        in_specs=[pl.BlockSpec((1, heads, hd), lambda b, s, pt: (b, 0, 0))]
                 + [page_spec(c) for c in range(SGATE_PAGES)],
        out_specs=pl.BlockSpec((1, heads, LANES), lambda b, s, pt: (b, 0, 0)),
        scratch_shapes=[pltpu.VMEM((steps, heads, SGATE_PAGES // ppb), F32)],
    )
    return pl.pallas_call(
        functools.partial(_sgate_kernel, ppb=ppb),
        grid_spec=grid_spec,
        out_shape=jax.ShapeDtypeStruct((n, heads, LANES), jnp.int32),
        compiler_params=_cparams(("parallel", "arbitrary"), 40),
        name="sample_gate",
    )(page_table.reshape(-1), q3, *([cache_k] * SGATE_PAGES))


def _sattn_kernel(pt_ref, ix_ref, q_ref, kn_ref, vn_ref, *refs, npg):
    kp = refs[:npg]
    vp = refs[npg:2 * npg]
    o_ref = refs[2 * npg]
    q = q_ref[0, 0]
    s_new = jnp.sum(q * kn_ref[0, 0], axis=-1, keepdims=True)
    logits = [jnp.sum(kp[j][...] * q, axis=-1, keepdims=True) for j in range(npg)]
    mx = s_new
    for lg in logits:
        mx = jnp.maximum(mx, jnp.max(lg, axis=0, keepdims=True))
    p_new = jnp.exp(s_new - mx)
    den = p_new
    num = p_new * vn_ref[0, 0]
    for j in range(npg):
        p = jnp.exp(logits[j] - mx)
        den = den + jnp.sum(p, axis=0, keepdims=True)
        num = num + jnp.sum(p * vp[j][...], axis=0, keepdims=True)
    o_ref[0, 0] = num / den


def _sample_attention(q4, k4, v4, cache_k, cache_v, page_table, idx):
    n, heads, _, hd = q4.shape
    page = cache_k.shape[2]
    n_pages = page_table.shape[1]
    ppb = MOBA_BLOCK // page
    npg = MOBA_TOP_K * ppb

    def page_spec(j):
        r, c = divmod(j, ppb)
        return pl.BlockSpec(
            (None, None, page, hd),
            lambda b, h, pt, ix: (pt[b * n_pages + ix[(b * heads + h) * MOBA_TOP_K + r] * ppb + c], h, 0, 0))

    tok = pl.BlockSpec((1, 1, 1, hd), lambda b, h, pt, ix: (b, h, 0, 0))
    grid_spec = pltpu.PrefetchScalarGridSpec(
        num_scalar_prefetch=2,
        grid=(n, heads),
        in_specs=[tok, tok, tok] + [page_spec(j) for j in range(npg)] * 2,
        out_specs=tok,
    )
    return pl.pallas_call(
        functools.partial(_sattn_kernel, npg=npg),
        grid_spec=grid_spec,
        out_shape=jax.ShapeDtypeStruct((n, heads, 1, hd), F32),
        compiler_params=_cparams(("parallel", "parallel"), 16),
        name="sample_attention",
    )(page_table.reshape(-1), idx[:, :, :MOBA_TOP_K].reshape(-1), q4, k4, v4,
      *([cache_k] * npg), *([cache_v] * npg))


def _first_max_rows(v, row):
    mx = jnp.max(v, axis=0, keepdims=True)
    first = jnp.min(jnp.where(v == mx, row, v.shape[0]), axis=0, keepdims=True)
    return mx, row == first


def _outproj_kernel(x_ref, att_ref, gm_ref, wo_ref, g1_ref, b1_ref, wrh_ref, wrl_ref, rb_ref,
                    h_ref, hb_ref, gates_ref, *, alpha):
    aw = att_ref.shape[1]
    mix = _dot(att_ref[...], wo_ref[0:aw, :]) + _dot(gm_ref[...], wo_ref[aw:, :])
    h = _layer_norm_rows(alpha * x_ref[...] + mix, g1_ref[...], b1_ref[...])
    h_ref[...] = h
    hh = h.astype(BF16)
    hb_ref[...] = hh
    hl = (h - hh.astype(F32)).astype(BF16)
    logits = _dot_nt(wrh_ref[...], hh) + _dot_nt(wrh_ref[...], hl) + _dot_nt(wrl_ref[...], hh)
    scores = _sigmoid(logits)
    biased = scores + rb_ref[...]
    ne, tm = biased.shape
    per = ne // N_EXPERT_GROUPS
    sub = lax.broadcasted_iota(jnp.int32, (per, tm), 0)
    gscore = []
    for g in range(N_EXPERT_GROUPS):
        bg = biased[g * per:(g + 1) * per, :]
        m1, at1 = _first_max_rows(bg, sub)
        m2 = jnp.max(jnp.where(at1, -jnp.inf, bg), axis=0, keepdims=True)
        gscore.append(m1 + m2)
    allowed = []
    for g in range(N_EXPERT_GROUPS):
        rank = jnp.zeros((1, tm), jnp.int32)
        for o in range(N_EXPERT_GROUPS):
            if o < g:
                rank = rank + jnp.where(gscore[o] >= gscore[g], 1, 0)
            elif o > g:
                rank = rank + jnp.where(gscore[o] > gscore[g], 1, 0)
        allowed.append(jnp.where(rank < TOPK_GROUPS, biased[g * per:(g + 1) * per, :], NEG))
    v = jnp.concatenate(allowed, axis=0)
    row = lax.broadcasted_iota(jnp.int32, (ne, tm), 0)
    chosen = jnp.zeros((ne, tm), jnp.bool_)
    for _ in range(MOE_TOP_K):
        _, at = _first_max_rows(v, row)
        chosen = chosen | at
        v = jnp.where(at, -jnp.inf, v)
    w = jnp.where(chosen, scores, 0.0)
    w = w / jnp.sum(w, axis=0, keepdims=True) * ROUTED_SCALE
    shared = jnp.where(lax.broadcasted_iota(jnp.int32, (8, tm), 0) == 0, 1.0, 0.0)
    pad = jnp.concatenate([w, shared, jnp.zeros((LANES - ne - 8, tm), F32)], axis=0)
    gates_ref[...] = pad.T


def _split_bf16(a):
    hi = a.astype(BF16)
    return hi, (a - hi.astype(F32)).astype(BF16)


def _outproj(x, att, gm, w_o, g1, b1, w_router, router_bias, alpha, tm):
    m, dm = x.shape
    ne = w_router.shape[1]
    assert m % tm == 0 and ne + 8 <= LANES and ne % N_EXPERT_GROUPS == 0
    wrh, wrl = _split_bf16(w_router.T)
    row = lambda c: pl.BlockSpec((tm, c), lambda i: (i, 0))
    full = lambda a: pl.BlockSpec(a.shape, lambda i: (0,) * a.ndim)
    args = (x, att, gm, w_o, g1.reshape(1, dm), b1.reshape(1, dm), wrh, wrl, router_bias.reshape(ne, 1))
    return pl.pallas_call(
        functools.partial(_outproj_kernel, alpha=alpha),
        grid=(m // tm,),
        in_specs=[row(dm), row(att.shape[1]), row(gm.shape[1])] + [full(a) for a in args[3:]],
        out_specs=[row(dm), row(dm), row(LANES)],
        out_shape=[jax.ShapeDtypeStruct((m, dm), F32), jax.ShapeDtypeStruct((m, dm), BF16),
                   jax.ShapeDtypeStruct((m, LANES), F32)],
        compiler_params=_cparams(("parallel",), 40),
        name="outproj_router",
    )(*args)


def _moe_kernel(h_ref, hb_ref, gates_ref, wg_ref, wu_ref, wd_ref, g2_ref, b2_ref, o_ref, acc_ref, *, alpha):
    e = pl.program_id(1)
    x = hb_ref[...]
    hg = _dot(x, wg_ref[0])
    hu = _dot(x, wu_ref[0])
    gates = gates_ref[...]
    lane = lax.broadcasted_iota(jnp.int32, gates.shape, 1)
    gcol = jnp.sum(jnp.where(lane == e, gates, 0.0), axis=-1, keepdims=True)
    a = (hg * _sigmoid(hg)) * hu * gcol
    y = _dot(a.astype(BF16), wd_ref[0])

    @pl.when(e == 0)
    def _():
        acc_ref[...] = y

    @pl.when(e > 0)
    def _():
        acc_ref[...] += y

    @pl.when(e == pl.num_programs(1) - 1)
    def _():
        o_ref[...] = _layer_norm_rows(alpha * h_ref[...] + acc_ref[...], g2_ref[...], b2_ref[...])


def _moe(h, hb, gates, wg, wu, wd, g2, b2, alpha, tm):
    m, dm = h.shape
    ne1, _, ed = wg.shape
    assert m % tm == 0
    row = lambda c: pl.BlockSpec((tm, c), lambda i, e: (i, 0))
    vec = pl.BlockSpec((1, dm), lambda i, e: (0, 0))
    return pl.pallas_call(
        functools.partial(_moe_kernel, alpha=alpha),
        grid=(m // tm, ne1),
        in_specs=[row(dm), row(dm), row(LANES),
                  pl.BlockSpec((1, dm, ed), lambda i, e: (e, 0, 0)),
                  pl.BlockSpec((1, dm, ed), lambda i, e: (e, 0, 0)),
                  pl.BlockSpec((1, ed, dm), lambda i, e: (e, 0, 0)), vec, vec],
        out_specs=row(dm),
        out_shape=jax.ShapeDtypeStruct((m, dm), F32),
        scratch_shapes=[pltpu.VMEM((tm, dm), F32)],
        compiler_params=_cparams(("parallel", "arbitrary"), 48),
        name="moe_dense",
    )(h, hb, gates, wg, wu, wd, g2.reshape(1, dm), b2.reshape(1, dm))


def kernel(x_prompt, x_sample, cache_k, cache_v, page_table, w_in, w_o, gm_ln_g, gm_ln_b, gm_w_s, gm_b_s,
           ln1_g, ln1_b, w_router, router_bias, w_gate, w_up, w_down, ws_gate, ws_up, ws_down, ln2_g, ln2_b):
    depth = w_in.shape[0]
    heads, page, hd = cache_k.shape[2], cache_k.shape[3], cache_k.shape[4]
    n, t, dm = x_prompt.shape
    ns, ts, _ = x_sample.shape
    assert ts == 1 and MOBA_BLOCK % page == 0 and MOBA_BLOCK == 1 << BLOCK_SHIFT
    alpha = float((2 * depth) ** 0.25)
    xp, xs = x_prompt.reshape(n * t, dm), x_sample.reshape(ns, dm)
    outs = [[] for _ in range(5)]
    for l in range(depth):
        w_in_b = w_in[l].astype(BF16)
        w_o_b = w_o[l].astype(BF16)
        wg = jnp.concatenate([w_gate[l], ws_gate[l][None]], axis=0).astype(BF16)
        wu = jnp.concatenate([w_up[l], ws_up[l][None]], axis=0).astype(BF16)
        wd = jnp.concatenate([w_down[l], ws_down[l][None]], axis=0).astype(BF16)
        gm_w = (gm_ln_g[l], gm_ln_b[l], gm_w_s[l], gm_b_s[l])

        def block_output(x, att, gm, tm_o, tm_e):
            h, hb, gates = _outproj(x, att, gm, w_o_b, ln1_g[l], ln1_b[l], w_router[l], router_bias[l], alpha, tm_o)
            return _moe(h, hb, gates, wg, wu, wd, ln2_g[l], ln2_b[l], alpha, tm_e)

        qp, kp, vp, k_pages, v_pages, gm = _proj_prompt(xp.reshape(n, t, dm), w_in_b, *gm_w, heads, hd, page)
        att = _moba_prompt(qp, kp, vp, hd)
        xp = block_output(xp, att.reshape(n * t, heads * hd), gm.reshape(n * t, -1), 512, 1024)
        outs[0].append(k_pages)
        outs[1].append(v_pages)

        q_s, k_s, v_s, gm_s, vn_s = _proj_sample(xs, w_in_b, *gm_w, heads, hd)
        idx = _sample_gate(q_s.reshape(ns, heads, hd), cache_k[l], page_table)
        shp4 = (ns, heads, 1, hd)
        att_s = _sample_attention(q_s.reshape(shp4), k_s.reshape(shp4), v_s.reshape(shp4),
                                  cache_k[l], cache_v[l], page_table, idx)
        xs = block_output(xs, att_s.reshape(ns, heads * hd).astype(BF16), gm_s, ns, ns)
        outs[2].append(k_s.reshape(shp4))
        outs[3].append(v_s.reshape(shp4))
        outs[4].append(vn_s.reshape(ns, 1, -1))
    return (xp.reshape(n, t, dm), xs.reshape(ns, 1, dm)) + tuple(jnp.stack(o) for o in outs)
```

```python
import functools

import jax
import jax.numpy as jnp
from jax import lax
from jax.experimental import pallas as pl
from jax.experimental.pallas import tpu as pltpu

F32 = jnp.float32
BF16 = jnp.bfloat16

MOBA_BLOCK = 256
BLOCK_SHIFT = 8
MOBA_TOP_K = 3
MOBA_QBLOCKS = 4
MOE_TOP_K = 8
N_EXPERT_GROUPS = 8
TOPK_GROUPS = 4
ROUTED_SCALE = 2.5
LN_EPS = 1e-5
NEG = -1e30
MASK_BIG = float(2.0 ** 100)
LANES = 128
V7X_VMEM_BYTES = 64 * 1024 * 1024


def _cparams(semantics, vmem_mb):
    assert vmem_mb * 1024 * 1024 < V7X_VMEM_BYTES
    return pltpu.CompilerParams(dimension_semantics=semantics, vmem_limit_bytes=vmem_mb * 1024 * 1024)


def _dot(a, b):
    return jnp.dot(a, b, preferred_element_type=F32)


def _dot_nt(a, b):
    return lax.dot_general(a, b, (((1,), (1,)), ((), ())), preferred_element_type=F32)


def _layer_norm_rows(y, g, b):
    mu = jnp.mean(y, axis=-1, keepdims=True)
    var = jnp.mean(jnp.square(y - mu), axis=-1, keepdims=True)
    return (y - mu) * lax.rsqrt(var + LN_EPS) * g + b


def _sigmoid(x):
    return 1.0 / (1.0 + jnp.exp(-x))


def _gmlp_norm(zgv, lng, lnb, groups, gdim):
    vg = jax.nn.gelu(zgv)
    out = []
    for g in range(groups):
        sl = slice(g * gdim, (g + 1) * gdim)
        out.append(_layer_norm_rows(vg[:, sl], lng[:, sl], lnb[:, sl]))
    return out


def _proj_prompt_kernel(x_ref, w_ref, lng_ref, lnb_ref, ws_ref, bias_ref,
                        q_ref, k_ref, v_ref, ko_ref, vo_ref, gm_ref, *, heads, hd, groups, chunk):
    tm = x_ref.shape[1]
    aw = heads * hd
    gdim = gm_ref.shape[2] // groups
    x = x_ref[0].astype(BF16)
    zq = _dot(x, w_ref[:, 0:aw])
    zk = _dot(x, w_ref[:, aw:2 * aw])
    zv = _dot(x, w_ref[:, 2 * aw:3 * aw])

    lane = lax.broadcasted_iota(jnp.int32, (tm, LANES), 1)
    row = lax.broadcasted_iota(jnp.int32, (tm, LANES), 0)
    blk = (pl.program_id(1) * tm + row) >> BLOCK_SHIFT
    lo = lane < hd
    for p in range(heads // 2):
        sl = slice(p * LANES, (p + 1) * LANES)
        cq = zq[:, sl] * (hd ** -0.5)
        ck = zk[:, sl]
        cv = zv[:, sl]
        q_ref[0, 2 * p] = jnp.where(lo, cq, 0.0).astype(BF16)
        q_ref[0, 2 * p + 1] = jnp.where(lo, 0.0, cq).astype(BF16)
        k_ref[0, 2 * p] = jnp.where(lo, ck, jnp.where(lane - hd == blk, -MASK_BIG, 0.0)).astype(BF16)
        k_ref[0, 2 * p + 1] = jnp.where(lo, jnp.where(lane == blk, -MASK_BIG, 0.0), ck).astype(BF16)
        v_ref[0, 2 * p] = jnp.where(lo, cv, jnp.where(lane == hd, 1.0, 0.0)).astype(BF16)
        v_ref[0, 2 * p + 1] = jnp.where(lo, jnp.where(lane == 0, 1.0, 0.0), cv).astype(BF16)

    page = ko_ref.shape[3]
    for c in range(tm // page):
        for h in range(heads):
            ko_ref[0, c, h] = zk[c * page:(c + 1) * page, h * hd:(h + 1) * hd]
            vo_ref[0, c, h] = zv[c * page:(c + 1) * page, h * hd:(h + 1) * hd]

    gw = groups * gdim
    u = jax.nn.gelu(_dot(x, w_ref[:, 3 * aw:3 * aw + gw]))
    vn = _gmlp_norm(_dot(x, w_ref[:, 3 * aw + gw:3 * aw + 2 * gw]), lng_ref[...], lnb_ref[...], groups, gdim)
    ci = lax.broadcasted_iota(jnp.int32, (chunk, chunk), 0)
    cj = lax.broadcasted_iota(jnp.int32, (chunk, chunk), 1)
    for g in range(groups):
        wm = jnp.where(ci >= cj, ws_ref[g], 0.0).astype(BF16)
        gs = slice(g * gdim, (g + 1) * gdim)
        for c in range(tm // chunk):
            rs = slice(c * chunk, (c + 1) * chunk)
            s = _dot(wm, vn[g][rs].astype(BF16)) + bias_ref[:, gs]
            gm_ref[0, rs, gs] = (u[rs, gs] * s).astype(BF16)


def _proj_sample_kernel(x_ref, w_ref, lng_ref, lnb_ref, w00_ref, b0_ref,
                        q_ref, k_ref, v_ref, gm_ref, vn_ref, *, heads, hd, groups):
    aw = heads * hd
    gw = vn_ref.shape[1]
    gdim = gw // groups
    x = x_ref[...].astype(BF16)
    q_ref[...] = _dot(x, w_ref[:, 0:aw]) * (hd ** -0.5)
    k_ref[...] = _dot(x, w_ref[:, aw:2 * aw])
    v_ref[...] = _dot(x, w_ref[:, 2 * aw:3 * aw])
    u = jax.nn.gelu(_dot(x, w_ref[:, 3 * aw:3 * aw + gw]))
    vn = _gmlp_norm(_dot(x, w_ref[:, 3 * aw + gw:3 * aw + 2 * gw]), lng_ref[...], lnb_ref[...], groups, gdim)
    for g in range(groups):
        gs = slice(g * gdim, (g + 1) * gdim)
        vn_ref[:, gs] = vn[g]
        gm_ref[:, gs] = (u[:, gs] * (w00_ref[:, gs] * vn[g] + b0_ref[:, gs])).astype(BF16)


def _proj_prompt(x, w_in, lng, lnb, w_s, b_s, heads, hd, page):
    n, t, dm = x.shape
    groups, chunk = w_s.shape[0], w_s.shape[1]
    gw = lng.shape[-1]
    gdim = gw // groups
    tm = 512
    assert t % tm == 0 and tm % MOBA_BLOCK == 0 and tm % page == 0 and tm % chunk == 0
    assert heads % 2 == 0 and 2 * hd == LANES and t // MOBA_BLOCK <= hd
    bias = jnp.repeat(b_s.T, gdim, axis=1)
    att_shape = jax.ShapeDtypeStruct((n, heads, t, LANES), BF16)
    page_shape = jax.ShapeDtypeStruct((n, t // page, heads, page, hd), F32)
    att_spec = pl.BlockSpec((1, heads, tm, LANES), lambda b, i: (b, 0, i, 0))
    page_spec = pl.BlockSpec((1, tm // page, heads, page, hd), lambda b, i: (b, i, 0, 0, 0))
    full = lambda a: pl.BlockSpec(a.shape, lambda b, i: (0,) * a.ndim)
    lng2, lnb2 = lng.reshape(1, gw), lnb.reshape(1, gw)
    return pl.pallas_call(
        functools.partial(_proj_prompt_kernel, heads=heads, hd=hd, groups=groups, chunk=chunk),
        grid=(n, t // tm),
        in_specs=[pl.BlockSpec((1, tm, dm), lambda b, i: (b, i, 0)), full(w_in), full(lng2), full(lnb2),
                  full(w_s), full(bias)],
        out_specs=[att_spec, att_spec, att_spec, page_spec, page_spec,
                   pl.BlockSpec((1, tm, gw), lambda b, i: (b, i, 0))],
        out_shape=[att_shape, att_shape, att_shape, page_shape, page_shape,
                   jax.ShapeDtypeStruct((n, t, gw), BF16)],
        compiler_params=_cparams(("parallel", "parallel"), 48),
        name="proj_prompt",
    )(x, w_in, lng2, lnb2, w_s, bias)


def _proj_sample(x, w_in, lng, lnb, w_s, b_s, heads, hd):
    m, dm = x.shape
    groups = w_s.shape[0]
    gw = lng.shape[-1]
    gdim = gw // groups
    aw = heads * hd
    w00 = jnp.repeat(w_s[:, 0, 0], gdim).reshape(1, gw)
    b0 = jnp.repeat(b_s[:, 0], gdim).reshape(1, gw)
    f = lambda c, dt: jax.ShapeDtypeStruct((m, c), dt)
    return pl.pallas_call(
        functools.partial(_proj_sample_kernel, heads=heads, hd=hd, groups=groups),
        out_shape=[f(aw, F32), f(aw, F32), f(aw, F32), f(gw, BF16), f(gw, F32)],
        compiler_params=_cparams(None, 32),
        name="proj_sample",
    )(x, w_in, lng.reshape(1, gw), lnb.reshape(1, gw), w00, b0)


def _moba_kernel(q_ref, k_ref, v_ref, o_ref, kmh_ref, kml_ref, q2_ref, m_ref, acc_ref, *, hd):
    t = q_ref.shape[2]
    bq = MOBA_BLOCK
    nb = t // bq
    rows = MOBA_QBLOCKS * bq
    nbp = -(-nb // 8) * 8
    aux0 = (hd, 0)

    key_blk = lax.broadcasted_iota(jnp.int32, (LANES, t), 1) >> BLOCK_SHIFT
    out_row = lax.broadcasted_iota(jnp.int32, (LANES, t), 0)
    lane_sq = lax.broadcasted_iota(jnp.int32, (LANES, LANES), 1)
    for hh in range(2):
        avg = jnp.where(out_row - aux0[hh] == key_blk, 1.0 / bq, 0.0).astype(BF16)
        km_rows = _dot(avg, k_ref[0, hh])
        km_rows = jnp.where(lane_sq < hd if hh == 0 else lane_sq >= hd, km_rows, 0.0)
        hi = km_rows.astype(BF16)
        kmh_ref[hh] = hi
        kml_ref[hh] = (km_rows - hi.astype(F32)).astype(BF16)

    lane = lax.broadcasted_iota(jnp.int32, (rows, LANES), 1)
    g_blk = lax.broadcasted_iota(jnp.int32, (nbp, rows), 0)
    g_qblk = lax.broadcasted_iota(jnp.int32, (nbp, rows), 1) >> BLOCK_SHIFT

    def update(hh, j, r_lo, causal):
        c0 = pl.multiple_of(j * bq, bq)
        s = _dot_nt(q2_ref[hh, r_lo:, :], k_ref[0, hh, pl.ds(c0, bq), :])
        if causal:
            rr = lax.broadcasted_iota(jnp.int32, s.shape, 0)
            cc = lax.broadcasted_iota(jnp.int32, s.shape, 1)
            s = jnp.where(cc <= rr, s, NEG)
        m_prev = m_ref[hh, r_lo:, :]
        m_new = jnp.maximum(m_prev, jnp.max(s, axis=-1, keepdims=True))
        alpha = jnp.exp(m_prev - m_new)
        p = jnp.exp(s - jnp.concatenate([m_new] * (bq // LANES), axis=1))
        m_ref[hh, r_lo:, :] = m_new
        acc_ref[hh, r_lo:, :] = alpha * acc_ref[hh, r_lo:, :] + _dot(p.astype(BF16), v_ref[0, hh, pl.ds(c0, bq), :])

    def q_rows(sb, carry):
        r0 = pl.multiple_of(sb * rows, rows)
        b0 = sb * MOBA_QBLOCKS
        for hh in range(2):
            q = q_ref[0, hh, pl.ds(r0, rows), :]
            gate = _dot_nt(kmh_ref[hh], q) + _dot_nt(kml_ref[hh], q)
            past = g_blk < b0 + g_qblk
            g = jnp.where(past, gate[aux0[hh]:aux0[hh] + nbp, :], NEG)
            notsel = jnp.where(past, 1.0, 0.0)
            for _ in range(MOBA_TOP_K):
                mx, at = _first_max_rows(g, g_blk)
                pick = at & (mx > 0.5 * NEG)
                notsel = jnp.where(pick, 0.0, notsel)
                g = jnp.where(pick, NEG, g)
            pieces = [jnp.zeros((aux0[hh], rows), F32)] if aux0[hh] else []
            pieces += [notsel, jnp.zeros((LANES - aux0[hh] - nbp, rows), F32)]
            q2_ref[hh] = q + jnp.concatenate(pieces, axis=0).T.astype(BF16)
            m_ref[hh] = jnp.full((rows, LANES), NEG, F32)
            acc_ref[hh] = jnp.zeros((rows, LANES), F32)

        def past_blocks(g, c):
            for d in range(MOBA_QBLOCKS):
                for hh in range(2):
                    update(hh, g * MOBA_QBLOCKS + d, 0, False)
            return c

        lax.fori_loop(0, sb, past_blocks, 0)
        for d in range(MOBA_QBLOCKS):
            for hh in range(2):
                update(hh, b0 + d, d * bq, True)
        a0 = acc_ref[0]
        a1 = acc_ref[1]
        o0 = a0 * (1.0 / a0[:, hd:hd + 1])
        o1 = a1 * (1.0 / a1[:, 0:1])
        o_ref[0, pl.ds(r0, rows), :] = jnp.where(lane < hd, o0, o1).astype(o_ref.dtype)
        return carry

    lax.fori_loop(0, nb // MOBA_QBLOCKS, q_rows, 0)


def _moba_prompt(qp, kp, vp, hd):
    n, heads, t, _ = qp.shape
    rows = MOBA_QBLOCKS * MOBA_BLOCK
    assert t % rows == 0 and MOBA_BLOCK % LANES == 0
    spec = pl.BlockSpec((1, 2, t, LANES), lambda b, p: (b, p, 0, 0))
    return pl.pallas_call(
        functools.partial(_moba_kernel, hd=hd),
        grid=(n, heads // 2),
        in_specs=[spec, spec, spec],
        out_specs=pl.BlockSpec((1, t, LANES), lambda b, p: (b, 0, p)),
        out_shape=jax.ShapeDtypeStruct((n, t, heads * hd), BF16),
        scratch_shapes=[pltpu.VMEM((2, LANES, LANES), BF16), pltpu.VMEM((2, LANES, LANES), BF16),
                        pltpu.VMEM((2, rows, LANES), BF16),
                        pltpu.VMEM((2, rows, LANES), F32), pltpu.VMEM((2, rows, LANES), F32)],
        compiler_params=_cparams(("parallel", "parallel"), 48),
        name="moba_prompt",
    )(qp, kp, vp)


SGATE_PAGES = 16


def _sgate_kernel(pt_ref, q_ref, *refs, ppb):
    pages = refs[:SGATE_PAGES]
    idx_ref = refs[SGATE_PAGES]
    gate_ref = refs[SGATE_PAGES + 1]
    s = pl.program_id(1)
    bps = SGATE_PAGES // ppb
    q = q_ref[0]
    rows = []
    for b in range(bps):
        acc = jnp.sum(pages[b * ppb][...], axis=1)
        for c in range(1, ppb):
            acc = acc + jnp.sum(pages[b * ppb + c][...], axis=1)
        km = acc / MOBA_BLOCK
        rows.append(jnp.sum(km * q, axis=-1, keepdims=True))
    gate_ref[s] = jnp.concatenate(rows, axis=1)

    @pl.when(s == pl.num_programs(1) - 1)
    def _():
        nsteps = gate_ref.shape[0]
        g = jnp.concatenate([gate_ref[i] for i in range(nsteps)], axis=1)
        col = lax.broadcasted_iota(jnp.int32, g.shape, 1)
        picks = []
        for _ in range(MOBA_TOP_K):
            mx = jnp.max(g, axis=-1, keepdims=True)
            first = jnp.min(jnp.where(g == mx, col, g.shape[1]), axis=-1, keepdims=True)
            picks.append(first)
            g = jnp.where(col == first, NEG, g)
        picks.append(jnp.zeros((g.shape[0], LANES - MOBA_TOP_K), jnp.int32))
        idx_ref[0] = jnp.concatenate(picks, axis=1)


def _sample_gate(q3, cache_k, layer, page_table):
    n, heads, hd = q3.shape
    page = cache_k.shape[3]
    n_pages = page_table.shape[1]
    ppb = MOBA_BLOCK // page
    assert n_pages % SGATE_PAGES == 0 and SGATE_PAGES % ppb == 0 and n_pages // ppb >= MOBA_TOP_K
    steps = n_pages // SGATE_PAGES

    def page_spec(c):
        return pl.BlockSpec((None, None, heads, page, hd),
                            lambda b, s, pt: (layer, pt[b * n_pages + s * SGATE_PAGES + c], 0, 0, 0))

    grid_spec = pltpu.PrefetchScalarGridSpec(
        num_scalar_prefetch=1,
        grid=(n, steps),
        in_specs=[pl.BlockSpec((1, heads, hd), lambda b, s, pt: (b, 0, 0))]
                 + [page_spec(c) for c in range(SGATE_PAGES)],
        out_specs=pl.BlockSpec((1, heads, LANES), lambda b, s, pt: (b, 0, 0)),
        scratch_shapes=[pltpu.VMEM((steps, heads, SGATE_PAGES // ppb), F32)],
    )
    return pl.pallas_call(
        functools.partial(_sgate_kernel, ppb=ppb),
        grid_spec=grid_spec,
        out_shape=jax.ShapeDtypeStruct((n, heads, LANES), jnp.int32),
        compiler_params=_cparams(("parallel", "arbitrary"), 40),
        name="sample_gate",
    )(page_table.reshape(-1), q3, *([cache_k] * SGATE_PAGES))


def _sattn_kernel(pt_ref, ix_ref, q_ref, kn_ref, vn_ref, *refs, npg):
    kp = refs[:npg]
    vp = refs[npg:2 * npg]
    o_ref = refs[2 * npg]
    q = q_ref[0, 0]
    s_new = jnp.sum(q * kn_ref[0, 0], axis=-1, keepdims=True)
    logits = [jnp.sum(kp[j][...] * q, axis=-1, keepdims=True) for j in range(npg)]
    mx = s_new
    for lg in logits:
        mx = jnp.maximum(mx, jnp.max(lg, axis=0, keepdims=True))
    p_new = jnp.exp(s_new - mx)
    den = p_new
    num = p_new * vn_ref[0, 0]
    for j in range(npg):
        p = jnp.exp(logits[j] - mx)
        den = den + jnp.sum(p, axis=0, keepdims=True)
        num = num + jnp.sum(p * vp[j][...], axis=0, keepdims=True)
    o_ref[0, 0] = num / den


def _sample_attention(q4, k4, v4, cache_k, cache_v, layer, page_table, idx):
    n, heads, _, hd = q4.shape
    page = cache_k.shape[3]
    n_pages = page_table.shape[1]
    ppb = MOBA_BLOCK // page
    npg = MOBA_TOP_K * ppb

    def page_spec(j):
        r, c = divmod(j, ppb)
        return pl.BlockSpec(
            (None, None, None, page, hd),
            lambda b, h, pt, ix: (layer, pt[b * n_pages + ix[(b * heads + h) * MOBA_TOP_K + r] * ppb + c], h, 0, 0))

    tok = pl.BlockSpec((1, 1, 1, hd), lambda b, h, pt, ix: (b, h, 0, 0))
    grid_spec = pltpu.PrefetchScalarGridSpec(
        num_scalar_prefetch=2,
        grid=(n, heads),
        in_specs=[tok, tok, tok] + [page_spec(j) for j in range(npg)] * 2,
        out_specs=tok,
    )
    return pl.pallas_call(
        functools.partial(_sattn_kernel, npg=npg),
        grid_spec=grid_spec,
        out_shape=jax.ShapeDtypeStruct((n, heads, 1, hd), F32),
        compiler_params=_cparams(("parallel", "parallel"), 16),
        name="sample_attention",
    )(page_table.reshape(-1), idx[:, :, :MOBA_TOP_K].reshape(-1), q4, k4, v4,
      *([cache_k] * npg), *([cache_v] * npg))


def _first_max_rows(v, row):
    mx = jnp.max(v, axis=0, keepdims=True)
    first = jnp.min(jnp.where(v == mx, row, v.shape[0]), axis=0, keepdims=True)
    return mx, row == first


def _outproj_kernel(x_ref, att_ref, gm_ref, wo_ref, g1_ref, b1_ref, wrh_ref, wrl_ref, rb_ref,
                    h_ref, hb_ref, gates_ref, *, alpha):
    aw = att_ref.shape[1]
    mix = _dot(att_ref[...], wo_ref[0:aw, :]) + _dot(gm_ref[...], wo_ref[aw:, :])
    h = _layer_norm_rows(alpha * x_ref[...] + mix, g1_ref[...], b1_ref[...])
    h_ref[...] = h
    hh = h.astype(BF16)
    hb_ref[...] = hh
    hl = (h - hh.astype(F32)).astype(BF16)
    logits = _dot_nt(wrh_ref[...], hh) + _dot_nt(wrh_ref[...], hl) + _dot_nt(wrl_ref[...], hh)
    scores = _sigmoid(logits)
    biased = scores + rb_ref[...]
    ne, tm = biased.shape
    per = ne // N_EXPERT_GROUPS
    sub = lax.broadcasted_iota(jnp.int32, (per, tm), 0)
    gscore = []
    for g in range(N_EXPERT_GROUPS):
        bg = biased[g * per:(g + 1) * per, :]
        m1, at1 = _first_max_rows(bg, sub)
        m2 = jnp.max(jnp.where(at1, -jnp.inf, bg), axis=0, keepdims=True)
        gscore.append(m1 + m2)
    allowed = []
    for g in range(N_EXPERT_GROUPS):
        rank = jnp.zeros((1, tm), jnp.int32)
        for o in range(N_EXPERT_GROUPS):
            if o < g:
                rank = rank + jnp.where(gscore[o] >= gscore[g], 1, 0)
            elif o > g:
                rank = rank + jnp.where(gscore[o] > gscore[g], 1, 0)
        allowed.append(jnp.where(rank < TOPK_GROUPS, biased[g * per:(g + 1) * per, :], NEG))
    v = jnp.concatenate(allowed, axis=0)
    row = lax.broadcasted_iota(jnp.int32, (ne, tm), 0)
    chosen = jnp.zeros((ne, tm), jnp.bool_)
    for _ in range(MOE_TOP_K):
        _, at = _first_max_rows(v, row)
        chosen = chosen | at
        v = jnp.where(at, -jnp.inf, v)
    w = jnp.where(chosen, scores, 0.0)
    w = w / jnp.sum(w, axis=0, keepdims=True) * ROUTED_SCALE
    shared = jnp.where(lax.broadcasted_iota(jnp.int32, (8, tm), 0) == 0, 1.0, 0.0)
    pad = jnp.concatenate([w, shared, jnp.zeros((LANES - ne - 8, tm), F32)], axis=0)
    gates_ref[...] = pad.T


def _split_bf16(a):
    hi = a.astype(BF16)
    return hi, (a - hi.astype(F32)).astype(BF16)


def _outproj(x, att, gm, w_o, g1, b1, w_router, router_bias, alpha, tm):
    m, dm = x.shape
    ne = w_router.shape[1]
    assert m % tm == 0 and ne + 8 <= LANES and ne % N_EXPERT_GROUPS == 0
    wrh, wrl = _split_bf16(w_router.T)
    row = lambda c: pl.BlockSpec((tm, c), lambda i: (i, 0))
    full = lambda a: pl.BlockSpec(a.shape, lambda i: (0,) * a.ndim)
    args = (x, att, gm, w_o, g1.reshape(1, dm), b1.reshape(1, dm), wrh, wrl, router_bias.reshape(ne, 1))
    return pl.pallas_call(
        functools.partial(_outproj_kernel, alpha=alpha),
        grid=(m // tm,),
        in_specs=[row(dm), row(att.shape[1]), row(gm.shape[1])] + [full(a) for a in args[3:]],
        out_specs=[row(dm), row(dm), row(LANES)],
        out_shape=[jax.ShapeDtypeStruct((m, dm), F32), jax.ShapeDtypeStruct((m, dm), BF16),
                   jax.ShapeDtypeStruct((m, LANES), F32)],
        compiler_params=_cparams(("parallel",), 40),
        name="outproj_router",
    )(*args)


def _moe_kernel(h_ref, hb_ref, gates_ref, wg_ref, wu_ref, wd_ref, g2_ref, b2_ref, o_ref, acc_ref, *, alpha):
    e = pl.program_id(1)
    x = hb_ref[...]
    hg = _dot(x, wg_ref[0])
    hu = _dot(x, wu_ref[0])
    gates = gates_ref[...]
    lane = lax.broadcasted_iota(jnp.int32, gates.shape, 1)
    gcol = jnp.sum(jnp.where(lane == e, gates, 0.0), axis=-1, keepdims=True)
    a = (hg * _sigmoid(hg)) * hu * gcol
    y = _dot(a.astype(BF16), wd_ref[0])

    @pl.when(e == 0)
    def _():
        acc_ref[...] = y

    @pl.when(e > 0)
    def _():
        acc_ref[...] += y

    @pl.when(e == pl.num_programs(1) - 1)
    def _():
        o_ref[...] = _layer_norm_rows(alpha * h_ref[...] + acc_ref[...], g2_ref[...], b2_ref[...])


def _moe(h, hb, gates, wg, wu, wd, g2, b2, alpha, tm):
    m, dm = h.shape
    ne1, _, ed = wg.shape
    assert m % tm == 0
    row = lambda c: pl.BlockSpec((tm, c), lambda i, e: (i, 0))
    vec = pl.BlockSpec((1, dm), lambda i, e: (0, 0))
    return pl.pallas_call(
        functools.partial(_moe_kernel, alpha=alpha),
        grid=(m // tm, ne1),
        in_specs=[row(dm), row(dm), row(LANES),
                  pl.BlockSpec((1, dm, ed), lambda i, e: (e, 0, 0)),
                  pl.BlockSpec((1, dm, ed), lambda i, e: (e, 0, 0)),
                  pl.BlockSpec((1, ed, dm), lambda i, e: (e, 0, 0)), vec, vec],
        out_specs=row(dm),
        out_shape=jax.ShapeDtypeStruct((m, dm), F32),
        scratch_shapes=[pltpu.VMEM((tm, dm), F32)],
        compiler_params=_cparams(("parallel", "arbitrary"), 48),
        name="moe_dense",
    )(h, hb, gates, wg, wu, wd, g2.reshape(1, dm), b2.reshape(1, dm))


def kernel(x_prompt, x_sample, cache_k, cache_v, page_table, w_in, w_o, gm_ln_g, gm_ln_b, gm_w_s, gm_b_s,
           ln1_g, ln1_b, w_router, router_bias, w_gate, w_up, w_down, ws_gate, ws_up, ws_down, ln2_g, ln2_b):
    depth = w_in.shape[0]
    heads, page, hd = cache_k.shape[2], cache_k.shape[3], cache_k.shape[4]
    n, t, dm = x_prompt.shape
    ns, ts, _ = x_sample.shape
    assert ts == 1 and MOBA_BLOCK % page == 0 and MOBA_BLOCK == 1 << BLOCK_SHIFT
    alpha = float((2 * depth) ** 0.25)
    xp, xs = x_prompt.reshape(n * t, dm), x_sample.reshape(ns, dm)
    outs = [[] for _ in range(5)]
    for l in range(depth):
        w_in_b = w_in[l].astype(BF16)
        w_o_b = w_o[l].astype(BF16)
        wg = jnp.concatenate([w_gate[l], ws_gate[l][None]], axis=0).astype(BF16)
        wu = jnp.concatenate([w_up[l], ws_up[l][None]], axis=0).astype(BF16)
        wd = jnp.concatenate([w_down[l], ws_down[l][None]], axis=0).astype(BF16)
        gm_w = (gm_ln_g[l], gm_ln_b[l], gm_w_s[l], gm_b_s[l])

        def block_output(x, att, gm, tm_o, tm_e):
            h, hb, gates = _outproj(x, att, gm, w_o_b, ln1_g[l], ln1_b[l], w_router[l], router_bias[l], alpha, tm_o)
            return _moe(h, hb, gates, wg, wu, wd, ln2_g[l], ln2_b[l], alpha, tm_e)

        qp, kp, vp, k_pages, v_pages, gm = _proj_prompt(xp.reshape(n, t, dm), w_in_b, *gm_w, heads, hd, page)
        att = _moba_prompt(qp, kp, vp, hd)
        xp = block_output(xp, att.reshape(n * t, heads * hd), gm.reshape(n * t, -1), 512, 1024)
        outs[0].append(k_pages)
        outs[1].append(v_pages)

        q_s, k_s, v_s, gm_s, vn_s = _proj_sample(xs, w_in_b, *gm_w, heads, hd)
        idx = _sample_gate(q_s.reshape(ns, heads, hd), cache_k, l, page_table)
        shp4 = (ns, heads, 1, hd)
        att_s = _sample_attention(q_s.reshape(shp4), k_s.reshape(shp4), v_s.reshape(shp4),
                                  cache_k, cache_v, l, page_table, idx)
        xs = block_output(xs, att_s.reshape(ns, heads * hd).astype(BF16), gm_s, ns, ns)
        outs[2].append(k_s.reshape(shp4))
        outs[3].append(v_s.reshape(shp4))
        outs[4].append(vn_s.reshape(ns, 1, -1))
    return (xp.reshape(n, t, dm), xs.reshape(ns, 1, dm)) + tuple(jnp.stack(o) for o in outs)
```

```python
import functools

import jax
import jax.numpy as jnp
from jax import lax
from jax.experimental import pallas as pl
from jax.experimental.pallas import tpu as pltpu

F32 = jnp.float32
BF16 = jnp.bfloat16

MOBA_BLOCK = 256
BLOCK_SHIFT = 8
MOBA_TOP_K = 3
MOBA_QBLOCKS = 4
MOE_TOP_K = 8
N_EXPERT_GROUPS = 8
TOPK_GROUPS = 4
ROUTED_SCALE = 2.5
LN_EPS = 1e-5
NEG = -1e30
MASK_BIG = float(2.0 ** 100)
LANES = 128
V7X_VMEM_BYTES = 64 * 1024 * 1024


def _cparams(semantics, vmem_mb):
    assert vmem_mb * 1024 * 1024 < V7X_VMEM_BYTES
    return pltpu.CompilerParams(dimension_semantics=semantics, vmem_limit_bytes=vmem_mb * 1024 * 1024)


def _dot(a, b):
    return jnp.dot(a, b, preferred_element_type=F32)


def _dot_nt(a, b):
    return lax.dot_general(a, b, (((1,), (1,)), ((), ())), preferred_element_type=F32)


def _layer_norm_rows(y, g, b):
    mu = jnp.mean(y, axis=-1, keepdims=True)
    var = jnp.mean(jnp.square(y - mu), axis=-1, keepdims=True)
    return (y - mu) * lax.rsqrt(var + LN_EPS) * g + b


def _sigmoid(x):
    return 1.0 / (1.0 + jnp.exp(-x))


def _gmlp_norm(zgv, lng, lnb, groups, gdim):
    vg = jax.nn.gelu(zgv)
    out = []
    for g in range(groups):
        sl = slice(g * gdim, (g + 1) * gdim)
        out.append(_layer_norm_rows(vg[:, sl], lng[:, sl], lnb[:, sl]))
    return out


def _proj_prompt_kernel(x_ref, w_ref, lng_ref, lnb_ref, ws_ref, bias_ref,
                        q_ref, k_ref, v_ref, ko_ref, vo_ref, gm_ref, *, heads, hd, groups, chunk):
    tm = x_ref.shape[1]
    aw = heads * hd
    gdim = gm_ref.shape[2] // groups
    x = x_ref[0].astype(BF16)
    zq = _dot(x, w_ref[:, 0:aw])
    zk = _dot(x, w_ref[:, aw:2 * aw])
    zv = _dot(x, w_ref[:, 2 * aw:3 * aw])

    lane = lax.broadcasted_iota(jnp.int32, (tm, LANES), 1)
    row = lax.broadcasted_iota(jnp.int32, (tm, LANES), 0)
    blk = (pl.program_id(1) * tm + row) >> BLOCK_SHIFT
    lo = lane < hd
    for p in range(heads // 2):
        sl = slice(p * LANES, (p + 1) * LANES)
        cq = zq[:, sl] * (hd ** -0.5)
        ck = zk[:, sl]
        cv = zv[:, sl]
        q_ref[0, 2 * p] = jnp.where(lo, cq, 0.0).astype(BF16)
        q_ref[0, 2 * p + 1] = jnp.where(lo, 0.0, cq).astype(BF16)
        k_ref[0, 2 * p] = jnp.where(lo, ck, jnp.where(lane - hd == blk, -MASK_BIG, 0.0)).astype(BF16)
        k_ref[0, 2 * p + 1] = jnp.where(lo, jnp.where(lane == blk, -MASK_BIG, 0.0), ck).astype(BF16)
        v_ref[0, 2 * p] = jnp.where(lo, cv, jnp.where(lane == hd, 1.0, 0.0)).astype(BF16)
        v_ref[0, 2 * p + 1] = jnp.where(lo, jnp.where(lane == 0, 1.0, 0.0), cv).astype(BF16)

    page = ko_ref.shape[3]
    for c in range(tm // page):
        for h in range(heads):
            ko_ref[0, c, h] = zk[c * page:(c + 1) * page, h * hd:(h + 1) * hd]
            vo_ref[0, c, h] = zv[c * page:(c + 1) * page, h * hd:(h + 1) * hd]

    gw = groups * gdim
    u = jax.nn.gelu(_dot(x, w_ref[:, 3 * aw:3 * aw + gw]))
    vn = _gmlp_norm(_dot(x, w_ref[:, 3 * aw + gw:3 * aw + 2 * gw]), lng_ref[...], lnb_ref[...], groups, gdim)
    ci = lax.broadcasted_iota(jnp.int32, (chunk, chunk), 0)
    cj = lax.broadcasted_iota(jnp.int32, (chunk, chunk), 1)
    for g in range(groups):
        wm = jnp.where(ci >= cj, ws_ref[g], 0.0).astype(BF16)
        gs = slice(g * gdim, (g + 1) * gdim)
        for c in range(tm // chunk):
            rs = slice(c * chunk, (c + 1) * chunk)
            s = _dot(wm, vn[g][rs].astype(BF16)) + bias_ref[:, gs]
            gm_ref[0, rs, gs] = (u[rs, gs] * s).astype(BF16)


def _proj_sample_kernel(x_ref, w_ref, lng_ref, lnb_ref, w00_ref, b0_ref,
                        q_ref, k_ref, v_ref, gm_ref, vn_ref, *, heads, hd, groups):
    aw = heads * hd
    gw = vn_ref.shape[1]
    gdim = gw // groups
    x = x_ref[...].astype(BF16)
    q_ref[...] = _dot(x, w_ref[:, 0:aw]) * (hd ** -0.5)
    k_ref[...] = _dot(x, w_ref[:, aw:2 * aw])
    v_ref[...] = _dot(x, w_ref[:, 2 * aw:3 * aw])
    u = jax.nn.gelu(_dot(x, w_ref[:, 3 * aw:3 * aw + gw]))
    vn = _gmlp_norm(_dot(x, w_ref[:, 3 * aw + gw:3 * aw + 2 * gw]), lng_ref[...], lnb_ref[...], groups, gdim)
    for g in range(groups):
        gs = slice(g * gdim, (g + 1) * gdim)
        vn_ref[:, gs] = vn[g]
        gm_ref[:, gs] = (u[:, gs] * (w00_ref[:, gs] * vn[g] + b0_ref[:, gs])).astype(BF16)


def _proj_prompt(x, w_in, lng, lnb, w_s, b_s, heads, hd, page):
    n, t, dm = x.shape
    groups, chunk = w_s.shape[0], w_s.shape[1]
    gw = lng.shape[-1]
    gdim = gw // groups
    tm = 512
    assert t % tm == 0 and tm % MOBA_BLOCK == 0 and tm % page == 0 and tm % chunk == 0
    assert heads % 2 == 0 and 2 * hd == LANES and t // MOBA_BLOCK <= hd
    bias = jnp.repeat(b_s.T, gdim, axis=1)
    att_shape = jax.ShapeDtypeStruct((n, heads, t, LANES), BF16)
    page_shape = jax.ShapeDtypeStruct((n, t // page, heads, page, hd), F32)
    att_spec = pl.BlockSpec((1, heads, tm, LANES), lambda b, i: (b, 0, i, 0))
    page_spec = pl.BlockSpec((1, tm // page, heads, page, hd), lambda b, i: (b, i, 0, 0, 0))
    full = lambda a: pl.BlockSpec(a.shape, lambda b, i: (0,) * a.ndim)
    lng2, lnb2 = lng.reshape(1, gw), lnb.reshape(1, gw)
    return pl.pallas_call(
        functools.partial(_proj_prompt_kernel, heads=heads, hd=hd, groups=groups, chunk=chunk),
        grid=(n, t // tm),
        in_specs=[pl.BlockSpec((1, tm, dm), lambda b, i: (b, i, 0)), full(w_in), full(lng2), full(lnb2),
                  full(w_s), full(bias)],
        out_specs=[att_spec, att_spec, att_spec, page_spec, page_spec,
                   pl.BlockSpec((1, tm, gw), lambda b, i: (b, i, 0))],
        out_shape=[att_shape, att_shape, att_shape, page_shape, page_shape,
                   jax.ShapeDtypeStruct((n, t, gw), BF16)],
        compiler_params=_cparams(("parallel", "parallel"), 48),
        name="proj_prompt",
    )(x, w_in, lng2, lnb2, w_s, bias)


def _proj_sample(x, w_in, lng, lnb, w_s, b_s, heads, hd):
    m, dm = x.shape
    groups = w_s.shape[0]
    gw = lng.shape[-1]
    gdim = gw // groups
    aw = heads * hd
    w00 = jnp.repeat(w_s[:, 0, 0], gdim).reshape(1, gw)
    b0 = jnp.repeat(b_s[:, 0], gdim).reshape(1, gw)
    f = lambda c, dt: jax.ShapeDtypeStruct((m, c), dt)
    return pl.pallas_call(
        functools.partial(_proj_sample_kernel, heads=heads, hd=hd, groups=groups),
        out_shape=[f(aw, F32), f(aw, F32), f(aw, F32), f(gw, BF16), f(gw, F32)],
        compiler_params=_cparams(None, 32),
        name="proj_sample",
    )(x, w_in, lng.reshape(1, gw), lnb.reshape(1, gw), w00, b0)


def _moba_kernel(q_ref, k_ref, v_ref, o_ref, kmh_ref, kml_ref, q2_ref, m_ref, acc_ref, *, hd):
    t = q_ref.shape[2]
    bq = MOBA_BLOCK
    nb = t // bq
    rows = MOBA_QBLOCKS * bq
    nbp = -(-nb // 8) * 8
    aux0 = (hd, 0)

    key_blk = lax.broadcasted_iota(jnp.int32, (LANES, t), 1) >> BLOCK_SHIFT
    out_row = lax.broadcasted_iota(jnp.int32, (LANES, t), 0)
    lane_sq = lax.broadcasted_iota(jnp.int32, (LANES, LANES), 1)
    for hh in range(2):
        avg = jnp.where(out_row - aux0[hh] == key_blk, 1.0 / bq, 0.0).astype(BF16)
        km_rows = _dot(avg, k_ref[0, hh])
        km_rows = jnp.where(lane_sq < hd if hh == 0 else lane_sq >= hd, km_rows, 0.0)
        hi = km_rows.astype(BF16)
        kmh_ref[hh] = hi
        kml_ref[hh] = (km_rows - hi.astype(F32)).astype(BF16)

    lane = lax.broadcasted_iota(jnp.int32, (rows, LANES), 1)
    g_blk = lax.broadcasted_iota(jnp.int32, (nbp, rows), 0)
    g_qblk = lax.broadcasted_iota(jnp.int32, (nbp, rows), 1) >> BLOCK_SHIFT

    def update(hh, j, r_lo, causal):
        c0 = pl.multiple_of(j * bq, bq)
        s = _dot_nt(q2_ref[hh, r_lo:, :], k_ref[0, hh, pl.ds(c0, bq), :])
        if causal:
            rr = lax.broadcasted_iota(jnp.int32, s.shape, 0)
            cc = lax.broadcasted_iota(jnp.int32, s.shape, 1)
            s = jnp.where(cc <= rr, s, NEG)
        m_prev = m_ref[hh, r_lo:, :]
        m_new = jnp.maximum(m_prev, jnp.max(s, axis=-1, keepdims=True))
        alpha = jnp.exp(m_prev - m_new)
        p = jnp.exp(s - jnp.concatenate([m_new] * (bq // LANES), axis=1))
        m_ref[hh, r_lo:, :] = m_new
        acc_ref[hh, r_lo:, :] = alpha * acc_ref[hh, r_lo:, :] + _dot(p.astype(BF16), v_ref[0, hh, pl.ds(c0, bq), :])

    def q_rows(sb, carry):
        r0 = pl.multiple_of(sb * rows, rows)
        b0 = sb * MOBA_QBLOCKS
        for hh in range(2):
            q = q_ref[0, hh, pl.ds(r0, rows), :]
            gate = _dot_nt(kmh_ref[hh], q) + _dot_nt(kml_ref[hh], q)
            past = g_blk < b0 + g_qblk
            g = jnp.where(past, gate[aux0[hh]:aux0[hh] + nbp, :], NEG)
            notsel = jnp.where(past, 1.0, 0.0)
            for _ in range(MOBA_TOP_K):
                mx, at = _first_max_rows(g, g_blk)
                pick = at & (mx > 0.5 * NEG)
                notsel = jnp.where(pick, 0.0, notsel)
                g = jnp.where(pick, NEG, g)
            pieces = [jnp.zeros((aux0[hh], rows), F32)] if aux0[hh] else []
            pieces += [notsel, jnp.zeros((LANES - aux0[hh] - nbp, rows), F32)]
            q2_ref[hh] = q + jnp.concatenate(pieces, axis=0).T.astype(BF16)
            m_ref[hh] = jnp.full((rows, LANES), NEG, F32)
            acc_ref[hh] = jnp.zeros((rows, LANES), F32)

        def past_blocks(g, c):
            for d in range(MOBA_QBLOCKS):
                for hh in range(2):
                    update(hh, g * MOBA_QBLOCKS + d, 0, False)
            return c

        lax.fori_loop(0, sb, past_blocks, 0)
        for d in range(MOBA_QBLOCKS):
            for hh in range(2):
                update(hh, b0 + d, d * bq, True)
        a0 = acc_ref[0]
        a1 = acc_ref[1]
        o0 = a0 * (1.0 / a0[:, hd:hd + 1])
        o1 = a1 * (1.0 / a1[:, 0:1])
        o_ref[0, pl.ds(r0, rows), :] = jnp.where(lane < hd, o0, o1).astype(o_ref.dtype)
        return carry

    lax.fori_loop(0, nb // MOBA_QBLOCKS, q_rows, 0)


def _moba_prompt(qp, kp, vp, hd):
    n, heads, t, _ = qp.shape
    rows = MOBA_QBLOCKS * MOBA_BLOCK
    assert t % rows == 0 and MOBA_BLOCK % LANES == 0
    spec = pl.BlockSpec((1, 2, t, LANES), lambda b, p: (b, p, 0, 0))
    return pl.pallas_call(
        functools.partial(_moba_kernel, hd=hd),
        grid=(n, heads // 2),
        in_specs=[spec, spec, spec],
        out_specs=pl.BlockSpec((1, t, LANES), lambda b, p: (b, 0, p)),
        out_shape=jax.ShapeDtypeStruct((n, t, heads * hd), BF16),
        scratch_shapes=[pltpu.VMEM((2, LANES, LANES), BF16), pltpu.VMEM((2, LANES, LANES), BF16),
                        pltpu.VMEM((2, rows, LANES), BF16),
                        pltpu.VMEM((2, rows, LANES), F32), pltpu.VMEM((2, rows, LANES), F32)],
        compiler_params=_cparams(("parallel", "parallel"), 48),
        name="moba_prompt",
    )(qp, kp, vp)


SGATE_PAGES = 16


def _sgate_kernel(pt_ref, q_ref, *refs, ppb):
    pages = refs[:SGATE_PAGES]
    idx_ref = refs[SGATE_PAGES]
    gate_ref = refs[SGATE_PAGES + 1]
    s = pl.program_id(1)
    bps = SGATE_PAGES // ppb
    qb = jnp.broadcast_to(q_ref[0], pages[0].shape)
    rows = []
    for b in range(bps):
        acc = pages[b * ppb][...]
        for c in range(1, ppb):
            acc = acc + pages[b * ppb + c][...]
        per_row = jnp.sum(acc * qb, axis=1)
        rows.append(jnp.sum(per_row, axis=1, keepdims=True) / MOBA_BLOCK)
    gate_ref[s] = jnp.concatenate(rows, axis=1)

    @pl.when(s == pl.num_programs(1) - 1)
    def _():
        nsteps = gate_ref.shape[0]
        g = jnp.concatenate([gate_ref[i] for i in range(nsteps)], axis=1)
        col = lax.broadcasted_iota(jnp.int32, g.shape, 1)
        picks = []
        for _ in range(MOBA_TOP_K):
            mx = jnp.max(g, axis=-1, keepdims=True)
            first = jnp.min(jnp.where(g == mx, col, g.shape[1]), axis=-1, keepdims=True)
            picks.append(first)
            g = jnp.where(col == first, NEG, g)
        picks.append(jnp.zeros((g.shape[0], LANES - MOBA_TOP_K), jnp.int32))
        idx_ref[0] = jnp.concatenate(picks, axis=1)


def _sample_gate(q4, cache_kt, layer, page_table):
    n, heads, hd, _ = q4.shape
    page = cache_kt.shape[4]
    n_pages = page_table.shape[1]
    ppb = MOBA_BLOCK // page
    assert n_pages % SGATE_PAGES == 0 and SGATE_PAGES % ppb == 0 and n_pages // ppb >= MOBA_TOP_K
    steps = n_pages // SGATE_PAGES

    def page_spec(c):
        return pl.BlockSpec((None, None, heads, hd, page),
                            lambda b, s, pt: (layer, pt[b * n_pages + s * SGATE_PAGES + c], 0, 0, 0))

    grid_spec = pltpu.PrefetchScalarGridSpec(
        num_scalar_prefetch=1,
        grid=(n, steps),
        in_specs=[pl.BlockSpec((1, heads, hd, 1), lambda b, s, pt: (b, 0, 0, 0))]
                 + [page_spec(c) for c in range(SGATE_PAGES)],
        out_specs=pl.BlockSpec((1, heads, LANES), lambda b, s, pt: (b, 0, 0)),
        scratch_shapes=[pltpu.VMEM((steps, heads, SGATE_PAGES // ppb), F32)],
    )
    return pl.pallas_call(
        functools.partial(_sgate_kernel, ppb=ppb),
        grid_spec=grid_spec,
        out_shape=jax.ShapeDtypeStruct((n, heads, LANES), jnp.int32),
        compiler_params=_cparams(("parallel", "arbitrary"), 40),
        name="sample_gate",
    )(page_table.reshape(-1), q4, *([cache_kt] * SGATE_PAGES))


def _sattn_kernel(pt_ref, ix_ref, q_ref, kn_ref, vn_ref, *refs, npg):
    kp = refs[:npg]
    vp = refs[npg:2 * npg]
    o_ref = refs[2 * npg]
    q = q_ref[0, 0]
    s_new = jnp.sum(q * kn_ref[0, 0], axis=0, keepdims=True)
    logits = [jnp.sum(kp[j][...] * q, axis=0, keepdims=True) for j in range(npg)]
    mx = s_new
    for lg in logits:
        mx = jnp.maximum(mx, jnp.max(lg, axis=1, keepdims=True))
    p_new = jnp.exp(s_new - mx)
    den = p_new
    num = p_new * vn_ref[0, 0]
    for j in range(npg):
        p = jnp.exp(logits[j] - mx)
        den = den + jnp.sum(p, axis=1, keepdims=True)
        num = num + jnp.sum(vp[j][...] * p, axis=1, keepdims=True)
    o_ref[0, 0] = num / den


def _sample_attention(q4, k4, v4, cache_kt, cache_vt, layer, page_table, idx):
    n, heads, hd, _ = q4.shape
    page = cache_kt.shape[4]
    n_pages = page_table.shape[1]
    ppb = MOBA_BLOCK // page
    npg = MOBA_TOP_K * ppb

    def page_spec(j):
        r, c = divmod(j, ppb)
        return pl.BlockSpec(
            (None, None, None, hd, page),
            lambda b, h, pt, ix: (layer, pt[b * n_pages + ix[(b * heads + h) * MOBA_TOP_K + r] * ppb + c], h, 0, 0))

    tok = pl.BlockSpec((1, 1, hd, 1), lambda b, h, pt, ix: (b, h, 0, 0))
    grid_spec = pltpu.PrefetchScalarGridSpec(
        num_scalar_prefetch=2,
        grid=(n, heads),
        in_specs=[tok, tok, tok] + [page_spec(j) for j in range(npg)] * 2,
        out_specs=tok,
    )
    return pl.pallas_call(
        functools.partial(_sattn_kernel, npg=npg),
        grid_spec=grid_spec,
        out_shape=jax.ShapeDtypeStruct((n, heads, hd, 1), F32),
        compiler_params=_cparams(("parallel", "parallel"), 16),
        name="sample_attention",
    )(page_table.reshape(-1), idx[:, :, :MOBA_TOP_K].reshape(-1), q4, k4, v4,
      *([cache_kt] * npg), *([cache_vt] * npg))


MOE_ROW_TILE = 512
MOE_TOKEN_TILE = 256
PLAN_ROWS = 32


def _first_max_rows(v, row):
    mx = jnp.max(v, axis=0, keepdims=True)
    first = jnp.min(jnp.where(v == mx, row, v.shape[0]), axis=0, keepdims=True)
    return mx, row == first


def _pack_bf16_pairs(a):
    c = a.shape[1] // 2
    bits = lax.bitcast_convert_type(a.astype(BF16).astype(F32), jnp.uint32)
    return (bits[:, :c] >> 16) | (bits[:, c:] & jnp.uint32(0xFFFF0000))


def _unpack_pairs_f32(u):
    lo = lax.bitcast_convert_type(u << 16, F32)
    hi = lax.bitcast_convert_type(u & jnp.uint32(0xFFFF0000), F32)
    return lo, hi


def _unpack_bf16_pairs(u):
    return jnp.concatenate(_unpack_pairs_f32(u), axis=1).astype(BF16)


def _swiglu(x, wg, wu, wd):
    hg = _dot(x, wg)
    return _dot(((hg * _sigmoid(hg)) * _dot(x, wu)).astype(BF16), wd)


def _outproj_kernel(x_ref, att_ref, gm_ref, wo_ref, g1_ref, b1_ref, wrh_ref, wrl_ref, rb_ref,
                    h_ref, hp_ref, gates_ref, plan_ref, cnt_ref, base_ref, *, alpha):
    aw = att_ref.shape[1]
    mix = _dot(att_ref[...], wo_ref[0:aw, :]) + _dot(gm_ref[...], wo_ref[aw:, :])
    h = _layer_norm_rows(alpha * x_ref[...] + mix, g1_ref[...], b1_ref[...])
    h_ref[...] = h
    hp_ref[...] = _pack_bf16_pairs(h)
    hh = h.astype(BF16)
    hl = (h - hh.astype(F32)).astype(BF16)
    logits = _dot_nt(wrh_ref[...], hh) + _dot_nt(wrh_ref[...], hl) + _dot_nt(wrl_ref[...], hh)
    scores = _sigmoid(logits)
    biased = scores + rb_ref[...]
    ne, tm = biased.shape
    per = ne // N_EXPERT_GROUPS
    sub = lax.broadcasted_iota(jnp.int32, (per, tm), 0)
    gscore = []
    for g in range(N_EXPERT_GROUPS):
        bg = biased[g * per:(g + 1) * per, :]
        m1, at1 = _first_max_rows(bg, sub)
        m2 = jnp.max(jnp.where(at1, -jnp.inf, bg), axis=0, keepdims=True)
        gscore.append(m1 + m2)
    allowed = []
    for g in range(N_EXPERT_GROUPS):
        rank = jnp.zeros((1, tm), jnp.int32)
        for o in range(N_EXPERT_GROUPS):
            if o < g:
                rank = rank + jnp.where(gscore[o] >= gscore[g], 1, 0)
            elif o > g:
                rank = rank + jnp.where(gscore[o] > gscore[g], 1, 0)
        allowed.append(jnp.where(rank < TOPK_GROUPS, biased[g * per:(g + 1) * per, :], NEG))
    v = jnp.concatenate(allowed, axis=0)
    row = lax.broadcasted_iota(jnp.int32, (ne, tm), 0)
    chosen = jnp.zeros((ne, tm), jnp.bool_)
    for _ in range(MOE_TOP_K):
        _, at = _first_max_rows(v, row)
        chosen = chosen | at
        v = jnp.where(at, -jnp.inf, v)
    w = jnp.where(chosen, scores, 0.0)
    w = w / jnp.sum(w, axis=0, keepdims=True) * ROUTED_SCALE
    gates_ref[...] = jnp.concatenate([w, jnp.zeros((LANES - ne, tm), F32)], axis=0).T

    @pl.when(pl.program_id(0) == 0)
    def _():
        base_ref[...] = jnp.zeros(base_ref.shape, F32)

    cf = jnp.where(chosen, 1.0, 0.0)
    cb = cf.astype(BF16)
    before = jnp.where(lax.broadcasted_iota(jnp.int32, (tm, tm), 0) < lax.broadcasted_iota(jnp.int32, (tm, tm), 1),
                       1.0, 0.0).astype(BF16)
    base = base_ref[...]
    pos = base[:, 0:1] + _dot(cb, before)
    base_ref[...] = base + jnp.sum(cf, axis=1, keepdims=True)
    cnt_ref[...] = base_ref[...]
    lower = jnp.where(lax.broadcasted_iota(jnp.int32, (ne, ne), 1) < lax.broadcasted_iota(jnp.int32, (ne, ne), 0),
                      1.0, 0.0).astype(BF16)
    slot = _dot(lower, cb)
    erow = row.astype(F32)
    prow = lax.broadcasted_iota(jnp.int32, (PLAN_ROWS, tm), 0)
    plan = jnp.zeros((PLAN_ROWS, tm), F32)
    for k in range(MOE_TOP_K):
        sel = chosen & (slot == k)
        for j, val in enumerate((erow, pos, w)):
            picked = jnp.sum(jnp.where(sel, val, 0.0), axis=0, keepdims=True)
            plan = jnp.where(prow == j * MOE_TOP_K + k, picked, plan)
    plan_ref[...] = jnp.concatenate([plan, jnp.zeros((LANES - PLAN_ROWS, tm), F32)], axis=0).T


def _split_bf16(a):
    hi = a.astype(BF16)
    return hi, (a - hi.astype(F32)).astype(BF16)


def _outproj(x, att, gm, w_o, g1, b1, w_router, router_bias, alpha, tm):
    m, dm = x.shape
    ne = w_router.shape[1]
    assert m % tm == 0 and ne <= LANES and ne % N_EXPERT_GROUPS == 0 and 3 * MOE_TOP_K <= PLAN_ROWS
    assert m < 1 << 24
    wrh, wrl = _split_bf16(w_router.T)
    row = lambda c: pl.BlockSpec((tm, c), lambda i: (i, 0))
    full = lambda a: pl.BlockSpec(a.shape, lambda i: (0,) * a.ndim)
    args = (x, att, gm, w_o, g1.reshape(1, dm), b1.reshape(1, dm), wrh, wrl, router_bias.reshape(ne, 1))
    return pl.pallas_call(
        functools.partial(_outproj_kernel, alpha=alpha),
        grid=(m // tm,),
        in_specs=[row(dm), row(att.shape[1]), row(gm.shape[1])] + [full(a) for a in args[3:]],
        out_specs=[row(dm), row(dm // 2), row(LANES), row(LANES), pl.BlockSpec((ne, LANES), lambda i: (0, 0))],
        out_shape=[jax.ShapeDtypeStruct((m, dm), F32), jax.ShapeDtypeStruct((m, dm // 2), jnp.uint32),
                   jax.ShapeDtypeStruct((m, LANES), F32), jax.ShapeDtypeStruct((m, LANES), F32),
                   jax.ShapeDtypeStruct((ne, LANES), F32)],
        scratch_shapes=[pltpu.VMEM((ne, LANES), F32)],
        compiler_params=_cparams(("arbitrary",), 40),
        name="outproj_router",
    )(*args)


def _with_rows(dest_hbm, bufs, sem, step, nsteps, per_step, body):
    def copy(st, sl):
        return pltpu.make_async_copy(dest_hbm.at[pl.ds(st * per_step, per_step)], bufs[sl], sem.at[sl])

    @pl.when(step == 0)
    def _():
        copy(0, 0).start()

    for sl in range(2):
        @pl.when(step % 2 == sl)
        def _():
            @pl.when(step + 1 < nsteps)
            def _():
                copy(step + 1, 1 - sl).start()

            copy(step, sl).wait()
            body(bufs[sl])


def _dispatch_kernel(off_ref, cnt_ref, dest_hbm, hp_ref, xs_hbm, rows0, rows1, zero_ref, sem_dest, sem_row, *, tr):
    step, nsteps = pl.program_id(0), pl.num_programs(0)
    td = hp_ref.shape[0]

    def row_copy(t, d):
        return pltpu.make_async_copy(hp_ref.at[pl.ds(t, 1)], xs_hbm.at[pl.ds(d, 1)], sem_row)

    def scatter(rows):
        def issue(t, c):
            for k in range(MOE_TOP_K):
                row_copy(t, rows[t * MOE_TOP_K + k]).start(priority=k % 2)
            return c

        lax.fori_loop(0, td, issue, 0)

    def drain(t, c):
        for k in range(MOE_TOP_K):
            row_copy(0, 0).wait()
        return c

    _with_rows(dest_hbm, (rows0, rows1), sem_dest, step, nsteps, td * MOE_TOP_K, scatter)
    lax.fori_loop(0, td, drain, 0)

    @pl.when(step == nsteps - 1)
    def _():
        zero_ref[...] = jnp.zeros(zero_ref.shape, zero_ref.dtype)

        def zero_copy(d):
            return pltpu.make_async_copy(zero_ref.at[pl.ds(0, 1)], xs_hbm.at[pl.ds(d, 1)], sem_row)

        def per_expert(e, c):
            lo = off_ref[e] + cnt_ref[e]
            hi = off_ref[e] + (cnt_ref[e] + tr - 1) // tr * tr
            lax.fori_loop(lo, hi, lambda d, c2: (zero_copy(d).start(), c2)[1], 0)
            lax.fori_loop(lo, hi, lambda d, c2: (zero_copy(0).wait(), c2)[1], 0)
            return c

        lax.fori_loop(0, cnt_ref.shape[0], per_expert, 0)


def _expert_kernel(te_ref, nu_ref, x_ref, wg_ref, wu_ref, wd_ref, y_ref):
    @pl.when(pl.program_id(0) < nu_ref[0])
    def _():
        x = _unpack_bf16_pairs(x_ref[...])
        y_ref[...] = _pack_bf16_pairs(_swiglu(x, wg_ref[...], wu_ref[...], wd_ref[...]))


def _combine_kernel(dest_hbm, h_ref, hp_ref, plan_ref, ys_hbm, wsg_ref, wsu_ref, wsd_ref, g2_ref, b2_ref,
                    o_ref, rows0, rows1, ybuf, sem_dest, sem_row, *, alpha):
    step, nsteps = pl.program_id(0), pl.num_programs(0)
    tc = h_ref.shape[0]
    half = hp_ref.shape[1]

    def row_copy(t, k, d):
        return pltpu.make_async_copy(ys_hbm.at[pl.ds(d, 1)], ybuf.at[pl.ds(t, 1), pl.ds(k * half, half)], sem_row)

    def gather(rows):
        def issue(t, c):
            for k in range(MOE_TOP_K):
                row_copy(t, k, rows[t * MOE_TOP_K + k]).start(priority=k % 2)
            return c

        lax.fori_loop(0, tc, issue, 0)

    def drain(t, c):
        for k in range(MOE_TOP_K):
            row_copy(0, k, 0).wait()
        return c

    _with_rows(dest_hbm, (rows0, rows1), sem_dest, step, nsteps, tc * MOE_TOP_K, gather)
    shared = _swiglu(_unpack_bf16_pairs(hp_ref[...]), wsg_ref[...], wsu_ref[...], wsd_ref[...])
    lax.fori_loop(0, tc, drain, 0)

    plan = plan_ref[...]
    lo_acc = jnp.zeros((tc, half), F32)
    hi_acc = jnp.zeros((tc, half), F32)
    for k in range(MOE_TOP_K):
        gate = plan[:, 2 * MOE_TOP_K + k:2 * MOE_TOP_K + k + 1]
        lo, hi = _unpack_pairs_f32(ybuf[:, k * half:(k + 1) * half])
        lo_acc = lo_acc + gate * lo
        hi_acc = hi_acc + gate * hi
    ffn = jnp.concatenate([lo_acc, hi_acc], axis=1) + shared
    o_ref[...] = _layer_norm_rows(alpha * h_ref[...] + ffn, g2_ref[...], b2_ref[...])


def _moe_sparse(h, hp, plan, cnt, wg, wu, wd, wsg, wsu, wsd, g2, b2, alpha):
    m, dm = h.shape
    ne, _, ed = wg.shape
    tr, tt = MOE_ROW_TILE, MOE_TOKEN_TILE
    assert m % tt == 0 and (tt * MOE_TOP_K) % 1024 == 0
    n_tiles = (m * MOE_TOP_K) // tr + ne
    rows = n_tiles * tr

    counts = cnt[:, 0].astype(jnp.int32)
    padded = (counts + tr - 1) // tr * tr
    ends = jnp.cumsum(padded)
    off = ends - padded
    n_used = (ends[-1:] // tr).astype(jnp.int32)
    tile_e = jnp.minimum(jnp.searchsorted(ends, jnp.arange(n_tiles, dtype=jnp.int32) * tr, side="right"),
                         ne - 1).astype(jnp.int32)
    dest = (off[plan[:, 0:MOE_TOP_K].astype(jnp.int32)]
            + plan[:, MOE_TOP_K:2 * MOE_TOP_K].astype(jnp.int32)).reshape(-1)

    row_scratch = [pltpu.SMEM((tt * MOE_TOP_K,), jnp.int32)] * 2
    sems = [pltpu.SemaphoreType.DMA((2,)), pltpu.SemaphoreType.DMA(())]
    any_spec = pl.BlockSpec(memory_space=pl.ANY)

    xs = pl.pallas_call(
        functools.partial(_dispatch_kernel, tr=tr),
        grid_spec=pltpu.PrefetchScalarGridSpec(
            num_scalar_prefetch=2, grid=(m // tt,),
            in_specs=[any_spec, pl.BlockSpec((tt, dm // 2), lambda i, *_: (i, 0))],
            out_specs=any_spec,
            scratch_shapes=row_scratch + [pltpu.VMEM((8, dm // 2), jnp.uint32)] + sems),
        out_shape=jax.ShapeDtypeStruct((rows, dm // 2), jnp.uint32),
        compiler_params=_cparams(("arbitrary",), 16),
        name="moe_dispatch",
    )(off, counts, dest, hp)

    used = lambda i, te, nu: (jnp.minimum(i, nu[0] - 1), 0)
    ys = pl.pallas_call(
        _expert_kernel,
        grid_spec=pltpu.PrefetchScalarGridSpec(
            num_scalar_prefetch=2, grid=(n_tiles,),
            in_specs=[pl.BlockSpec((tr, dm // 2), used),
                      pl.BlockSpec((None, dm, ed), lambda i, te, nu: (te[i], 0, 0)),
                      pl.BlockSpec((None, dm, ed), lambda i, te, nu: (te[i], 0, 0)),
                      pl.BlockSpec((None, ed, dm), lambda i, te, nu: (te[i], 0, 0))],
            out_specs=pl.BlockSpec((tr, dm // 2), used)),
        out_shape=jax.ShapeDtypeStruct((rows, dm // 2), jnp.uint32),
        compiler_params=_cparams(("arbitrary",), 32),
        name="moe_experts",
    )(tile_e, n_used, xs, wg, wu, wd)

    row = lambda c: pl.BlockSpec((tt, c), lambda i: (i, 0))
    full = lambda a: pl.BlockSpec(a.shape, lambda i: (0,) * a.ndim)
    g2r, b2r = g2.reshape(1, dm), b2.reshape(1, dm)
    return pl.pallas_call(
        functools.partial(_combine_kernel, alpha=alpha),
        grid=(m // tt,),
        in_specs=[any_spec, row(dm), row(dm // 2), row(LANES), any_spec,
                  full(wsg), full(wsu), full(wsd), full(g2r), full(b2r)],
        out_specs=row(dm),
        scratch_shapes=row_scratch + [pltpu.VMEM((tt, MOE_TOP_K * dm // 2), jnp.uint32)] + sems,
        out_shape=jax.ShapeDtypeStruct((m, dm), F32),
        compiler_params=_cparams(("arbitrary",), 32),
        name="moe_combine",
    )(dest, h, hp, plan, ys, wsg, wsu, wsd, g2r, b2r)


def _moe_dense_kernel(h_ref, hp_ref, gates_ref, wg_ref, wu_ref, wd_ref, wsg_ref, wsu_ref, wsd_ref, g2_ref, b2_ref,
                      o_ref, acc_ref, *, alpha):
    e = pl.program_id(0)
    x = _unpack_bf16_pairs(hp_ref[...])
    hg = _dot(x, wg_ref[...])
    hu = _dot(x, wu_ref[...])
    gates = gates_ref[...]
    lane = lax.broadcasted_iota(jnp.int32, gates.shape, 1)
    gcol = jnp.sum(jnp.where(lane == e, gates, 0.0), axis=-1, keepdims=True)
    y = _dot(((hg * _sigmoid(hg)) * hu * gcol).astype(BF16), wd_ref[...])

    @pl.when(e == 0)
    def _():
        acc_ref[...] = y

    @pl.when(e > 0)
    def _():
        acc_ref[...] += y

    @pl.when(e == pl.num_programs(0) - 1)
    def _():
        ffn = acc_ref[...] + _swiglu(x, wsg_ref[...], wsu_ref[...], wsd_ref[...])
        o_ref[...] = _layer_norm_rows(alpha * h_ref[...] + ffn, g2_ref[...], b2_ref[...])


def _moe_dense(h, hp, gates, wg, wu, wd, wsg, wsu, wsd, g2, b2, alpha):
    m, dm = h.shape
    ne, _, ed = wg.shape
    full = lambda a: pl.BlockSpec(a.shape, lambda e: (0,) * a.ndim)
    g2r, b2r = g2.reshape(1, dm), b2.reshape(1, dm)
    return pl.pallas_call(
        functools.partial(_moe_dense_kernel, alpha=alpha),
        grid=(ne,),
        in_specs=[full(h), full(hp), full(gates),
                  pl.BlockSpec((None, dm, ed), lambda e: (e, 0, 0)),
                  pl.BlockSpec((None, dm, ed), lambda e: (e, 0, 0)),
                  pl.BlockSpec((None, ed, dm), lambda e: (e, 0, 0)),
                  full(wsg), full(wsu), full(wsd), full(g2r), full(b2r)],
        out_specs=full(h),
        out_shape=jax.ShapeDtypeStruct((m, dm), F32),
        scratch_shapes=[pltpu.VMEM((m, dm), F32)],
        compiler_params=_cparams(("arbitrary",), 32),
        name="moe_dense",
    )(h, hp, gates, wg, wu, wd, wsg, wsu, wsd, g2r, b2r)


def kernel(x_prompt, x_sample, cache_k, cache_v, page_table, w_in, w_o, gm_ln_g, gm_ln_b, gm_w_s, gm_b_s,
           ln1_g, ln1_b, w_router, router_bias, w_gate, w_up, w_down, ws_gate, ws_up, ws_down, ln2_g, ln2_b):
    depth = w_in.shape[0]
    heads, page, hd = cache_k.shape[2], cache_k.shape[3], cache_k.shape[4]
    n, t, dm = x_prompt.shape
    ns, ts, _ = x_sample.shape
    assert ts == 1 and MOBA_BLOCK % page == 0 and MOBA_BLOCK == 1 << BLOCK_SHIFT
    alpha = float((2 * depth) ** 0.25)
    xp, xs = x_prompt.reshape(n * t, dm), x_sample.reshape(ns, dm)
    cache_kt, cache_vt = jnp.swapaxes(cache_k, 3, 4), jnp.swapaxes(cache_v, 3, 4)
    outs = [[] for _ in range(5)]
    for l in range(depth):
        w_in_b = w_in[l].astype(BF16)
        w_o_b = w_o[l].astype(BF16)
        experts = tuple(a[l].astype(BF16) for a in (w_gate, w_up, w_down, ws_gate, ws_up, ws_down))
        gm_w = (gm_ln_g[l], gm_ln_b[l], gm_w_s[l], gm_b_s[l])
        route = (w_o_b, ln1_g[l], ln1_b[l], w_router[l], router_bias[l], alpha)
        norm2 = (ln2_g[l], ln2_b[l], alpha)

        qp, kp, vp, k_pages, v_pages, gm = _proj_prompt(xp.reshape(n, t, dm), w_in_b, *gm_w, heads, hd, page)
        att = _moba_prompt(qp, kp, vp, hd)
        h, hp, _, plan, cnt = _outproj(xp, att.reshape(n * t, heads * hd), gm.reshape(n * t, -1), *route, 512)
        xp = _moe_sparse(h, hp, plan, cnt, *experts, *norm2)
        outs[0].append(k_pages)
        outs[1].append(v_pages)

        q_s, k_s, v_s, gm_s, vn_s = _proj_sample(xs, w_in_b, *gm_w, heads, hd)
        col4 = (ns, heads, hd, 1)
        idx = _sample_gate(q_s.reshape(col4), cache_kt, l, page_table)
        att_s = _sample_attention(q_s.reshape(col4), k_s.reshape(col4), v_s.reshape(col4),
                                  cache_kt, cache_vt, l, page_table, idx)
        h, hp, gates, _, _ = _outproj(xs, att_s.reshape(ns, heads * hd).astype(BF16), gm_s, *route, ns)
        xs = _moe_dense(h, hp, gates, *experts, *norm2)
        shp4 = (ns, heads, 1, hd)
        outs[2].append(k_s.reshape(shp4))
        outs[3].append(v_s.reshape(shp4))
        outs[4].append(vn_s.reshape(ns, 1, -1))
    return (xp.reshape(n, t, dm), xs.reshape(ns, 1, dm)) + tuple(jnp.stack(o) for o in outs)
```

```python
import functools

import jax
import jax.numpy as jnp
from jax import lax
from jax.experimental import pallas as pl
from jax.experimental.pallas import tpu as pltpu

F32 = jnp.float32
BF16 = jnp.bfloat16

MOBA_BLOCK = 256
BLOCK_SHIFT = 8
MOBA_TOP_K = 3
MOBA_QBLOCKS = 4
MOE_TOP_K = 8
N_EXPERT_GROUPS = 8
TOPK_GROUPS = 4
ROUTED_SCALE = 2.5
LN_EPS = 1e-5
NEG = -1e30
MASK_BIG = float(2.0 ** 100)
LANES = 128
V7X_VMEM_BYTES = 64 * 1024 * 1024


def _cparams(semantics, vmem_mb):
    assert vmem_mb * 1024 * 1024 < V7X_VMEM_BYTES
    return pltpu.CompilerParams(dimension_semantics=semantics, vmem_limit_bytes=vmem_mb * 1024 * 1024)


def _dot(a, b):
    return jnp.dot(a, b, preferred_element_type=F32)


def _dot_nt(a, b):
    return lax.dot_general(a, b, (((1,), (1,)), ((), ())), preferred_element_type=F32)


def _layer_norm_rows(y, g, b):
    mu = jnp.mean(y, axis=-1, keepdims=True)
    var = jnp.mean(jnp.square(y - mu), axis=-1, keepdims=True)
    return (y - mu) * lax.rsqrt(var + LN_EPS) * g + b


def _sigmoid(x):
    return 1.0 / (1.0 + jnp.exp(-x))


def _gmlp_norm(zgv, lng, lnb, groups, gdim):
    vg = jax.nn.gelu(zgv)
    out = []
    for g in range(groups):
        sl = slice(g * gdim, (g + 1) * gdim)
        out.append(_layer_norm_rows(vg[:, sl], lng[:, sl], lnb[:, sl]))
    return out


def _proj_prompt_kernel(x_ref, w_ref, lng_ref, lnb_ref, ws_ref, bias_ref,
                        q_ref, k_ref, v_ref, ko_ref, vo_ref, gm_ref, *, heads, hd, groups, chunk):
    tm = x_ref.shape[1]
    aw = heads * hd
    gdim = gm_ref.shape[2] // groups
    x = x_ref[0].astype(BF16)
    zq = _dot(x, w_ref[:, 0:aw])
    zk = _dot(x, w_ref[:, aw:2 * aw])
    zv = _dot(x, w_ref[:, 2 * aw:3 * aw])

    lane = lax.broadcasted_iota(jnp.int32, (tm, LANES), 1)
    row = lax.broadcasted_iota(jnp.int32, (tm, LANES), 0)
    blk = (pl.program_id(1) * tm + row) >> BLOCK_SHIFT
    lo = lane < hd
    for p in range(heads // 2):
        sl = slice(p * LANES, (p + 1) * LANES)
        cq = zq[:, sl] * (hd ** -0.5)
        ck = zk[:, sl]
        cv = zv[:, sl]
        q_ref[0, 2 * p] = jnp.where(lo, cq, 0.0).astype(BF16)
        q_ref[0, 2 * p + 1] = jnp.where(lo, 0.0, cq).astype(BF16)
        k_ref[0, 2 * p] = jnp.where(lo, ck, jnp.where(lane - hd == blk, -MASK_BIG, 0.0)).astype(BF16)
        k_ref[0, 2 * p + 1] = jnp.where(lo, jnp.where(lane == blk, -MASK_BIG, 0.0), ck).astype(BF16)
        v_ref[0, 2 * p] = jnp.where(lo, cv, jnp.where(lane == hd, 1.0, 0.0)).astype(BF16)
        v_ref[0, 2 * p + 1] = jnp.where(lo, jnp.where(lane == 0, 1.0, 0.0), cv).astype(BF16)

    page = ko_ref.shape[3]
    for c in range(tm // page):
        for h in range(heads):
            ko_ref[0, c, h] = zk[c * page:(c + 1) * page, h * hd:(h + 1) * hd]
            vo_ref[0, c, h] = zv[c * page:(c + 1) * page, h * hd:(h + 1) * hd]

    gw = groups * gdim
    u = jax.nn.gelu(_dot(x, w_ref[:, 3 * aw:3 * aw + gw]))
    vn = _gmlp_norm(_dot(x, w_ref[:, 3 * aw + gw:3 * aw + 2 * gw]), lng_ref[...], lnb_ref[...], groups, gdim)
    ci = lax.broadcasted_iota(jnp.int32, (chunk, chunk), 0)
    cj = lax.broadcasted_iota(jnp.int32, (chunk, chunk), 1)
    for g in range(groups):
        wm = jnp.where(ci >= cj, ws_ref[g], 0.0).astype(BF16)
        gs = slice(g * gdim, (g + 1) * gdim)
        for c in range(tm // chunk):
            rs = slice(c * chunk, (c + 1) * chunk)
            s = _dot(wm, vn[g][rs].astype(BF16)) + bias_ref[:, gs]
            gm_ref[0, rs, gs] = (u[rs, gs] * s).astype(BF16)


def _proj_sample_kernel(x_ref, w_ref, lng_ref, lnb_ref, w00_ref, b0_ref,
                        q_ref, k_ref, v_ref, gm_ref, vn_ref, *, heads, hd, groups):
    aw = heads * hd
    gw = vn_ref.shape[1]
    gdim = gw // groups
    x = x_ref[...].astype(BF16)
    q_ref[...] = _dot(x, w_ref[:, 0:aw]) * (hd ** -0.5)
    k_ref[...] = _dot(x, w_ref[:, aw:2 * aw])
    v_ref[...] = _dot(x, w_ref[:, 2 * aw:3 * aw])
    u = jax.nn.gelu(_dot(x, w_ref[:, 3 * aw:3 * aw + gw]))
    vn = _gmlp_norm(_dot(x, w_ref[:, 3 * aw + gw:3 * aw + 2 * gw]), lng_ref[...], lnb_ref[...], groups, gdim)
    for g in range(groups):
        gs = slice(g * gdim, (g + 1) * gdim)
        vn_ref[:, gs] = vn[g]
        gm_ref[:, gs] = (u[:, gs] * (w00_ref[:, gs] * vn[g] + b0_ref[:, gs])).astype(BF16)


def _proj_prompt(x, w_in, lng, lnb, w_s, b_s, heads, hd, page):
    n, t, dm = x.shape
    groups, chunk = w_s.shape[0], w_s.shape[1]
    gw = lng.shape[-1]
    gdim = gw // groups
    tm = 512
    assert t % tm == 0 and tm % MOBA_BLOCK == 0 and tm % page == 0 and tm % chunk == 0
    assert heads % 2 == 0 and 2 * hd == LANES and t // MOBA_BLOCK <= hd
    bias = jnp.repeat(b_s.T, gdim, axis=1)
    att_shape = jax.ShapeDtypeStruct((n, heads, t, LANES), BF16)
    page_shape = jax.ShapeDtypeStruct((n, t // page, heads, page, hd), F32)
    att_spec = pl.BlockSpec((1, heads, tm, LANES), lambda b, i: (b, 0, i, 0))
    page_spec = pl.BlockSpec((1, tm // page, heads, page, hd), lambda b, i: (b, i, 0, 0, 0))
    full = lambda a: pl.BlockSpec(a.shape, lambda b, i: (0,) * a.ndim)
    lng2, lnb2 = lng.reshape(1, gw), lnb.reshape(1, gw)
    return pl.pallas_call(
        functools.partial(_proj_prompt_kernel, heads=heads, hd=hd, groups=groups, chunk=chunk),
        grid=(n, t // tm),
        in_specs=[pl.BlockSpec((1, tm, dm), lambda b, i: (b, i, 0)), full(w_in), full(lng2), full(lnb2),
                  full(w_s), full(bias)],
        out_specs=[att_spec, att_spec, att_spec, page_spec, page_spec,
                   pl.BlockSpec((1, tm, gw), lambda b, i: (b, i, 0))],
        out_shape=[att_shape, att_shape, att_shape, page_shape, page_shape,
                   jax.ShapeDtypeStruct((n, t, gw), BF16)],
        compiler_params=_cparams(("parallel", "parallel"), 48),
        name="proj_prompt",
    )(x, w_in, lng2, lnb2, w_s, bias)


def _proj_sample(x, w_in, lng, lnb, w_s, b_s, heads, hd):
    m, dm = x.shape
    groups = w_s.shape[0]
    gw = lng.shape[-1]
    gdim = gw // groups
    aw = heads * hd
    w00 = jnp.repeat(w_s[:, 0, 0], gdim).reshape(1, gw)
    b0 = jnp.repeat(b_s[:, 0], gdim).reshape(1, gw)
    f = lambda c, dt: jax.ShapeDtypeStruct((m, c), dt)
    return pl.pallas_call(
        functools.partial(_proj_sample_kernel, heads=heads, hd=hd, groups=groups),
        out_shape=[f(aw, F32), f(aw, F32), f(aw, F32), f(gw, BF16), f(gw, F32)],
        compiler_params=_cparams(None, 32),
        name="proj_sample",
    )(x, w_in, lng.reshape(1, gw), lnb.reshape(1, gw), w00, b0)


def _moba_kernel(q_ref, k_ref, v_ref, o_ref, kmh_ref, kml_ref, q2_ref, m_ref, acc_ref, *, hd):
    t = q_ref.shape[2]
    bq = MOBA_BLOCK
    nb = t // bq
    rows = MOBA_QBLOCKS * bq
    nbp = -(-nb // 8) * 8
    aux0 = (hd, 0)

    key_blk = lax.broadcasted_iota(jnp.int32, (LANES, t), 1) >> BLOCK_SHIFT
    out_row = lax.broadcasted_iota(jnp.int32, (LANES, t), 0)
    lane_sq = lax.broadcasted_iota(jnp.int32, (LANES, LANES), 1)
    for hh in range(2):
        avg = jnp.where(out_row - aux0[hh] == key_blk, 1.0 / bq, 0.0).astype(BF16)
        km_rows = _dot(avg, k_ref[0, hh])
        km_rows = jnp.where(lane_sq < hd if hh == 0 else lane_sq >= hd, km_rows, 0.0)
        hi = km_rows.astype(BF16)
        kmh_ref[hh] = hi
        kml_ref[hh] = (km_rows - hi.astype(F32)).astype(BF16)

    lane = lax.broadcasted_iota(jnp.int32, (rows, LANES), 1)
    g_blk = lax.broadcasted_iota(jnp.int32, (nbp, rows), 0)
    g_qblk = lax.broadcasted_iota(jnp.int32, (nbp, rows), 1) >> BLOCK_SHIFT

    def update(hh, j, r_lo, causal):
        c0 = pl.multiple_of(j * bq, bq)
        s = _dot_nt(q2_ref[hh, r_lo:, :], k_ref[0, hh, pl.ds(c0, bq), :])
        if causal:
            rr = lax.broadcasted_iota(jnp.int32, s.shape, 0)
            cc = lax.broadcasted_iota(jnp.int32, s.shape, 1)
            s = jnp.where(cc <= rr, s, NEG)
        m_prev = m_ref[hh, r_lo:, :]
        m_new = jnp.maximum(m_prev, jnp.max(s, axis=-1, keepdims=True))
        alpha = jnp.exp(m_prev - m_new)
        p = jnp.exp(s - jnp.concatenate([m_new] * (bq // LANES), axis=1))
        m_ref[hh, r_lo:, :] = m_new
        acc_ref[hh, r_lo:, :] = alpha * acc_ref[hh, r_lo:, :] + _dot(p.astype(BF16), v_ref[0, hh, pl.ds(c0, bq), :])

    def q_rows(sb, carry):
        r0 = pl.multiple_of(sb * rows, rows)
        b0 = sb * MOBA_QBLOCKS
        for hh in range(2):
            q = q_ref[0, hh, pl.ds(r0, rows), :]
            gate = _dot_nt(kmh_ref[hh], q) + _dot_nt(kml_ref[hh], q)
            past = g_blk < b0 + g_qblk
            g = jnp.where(past, gate[aux0[hh]:aux0[hh] + nbp, :], NEG)
            notsel = jnp.where(past, 1.0, 0.0)
            for _ in range(MOBA_TOP_K):
                mx, at = _first_max_rows(g, g_blk)
                pick = at & (mx > 0.5 * NEG)
                notsel = jnp.where(pick, 0.0, notsel)
                g = jnp.where(pick, NEG, g)
            pieces = [jnp.zeros((aux0[hh], rows), F32)] if aux0[hh] else []
            pieces += [notsel, jnp.zeros((LANES - aux0[hh] - nbp, rows), F32)]
            q2_ref[hh] = q + jnp.concatenate(pieces, axis=0).T.astype(BF16)
            m_ref[hh] = jnp.full((rows, LANES), NEG, F32)
            acc_ref[hh] = jnp.zeros((rows, LANES), F32)

        def past_blocks(g, c):
            for d in range(MOBA_QBLOCKS):
                for hh in range(2):
                    update(hh, g * MOBA_QBLOCKS + d, 0, False)
            return c

        lax.fori_loop(0, sb, past_blocks, 0)
        for d in range(MOBA_QBLOCKS):
            for hh in range(2):
                update(hh, b0 + d, d * bq, True)
        a0 = acc_ref[0]
        a1 = acc_ref[1]
        o0 = a0 * (1.0 / a0[:, hd:hd + 1])
        o1 = a1 * (1.0 / a1[:, 0:1])
        o_ref[0, pl.ds(r0, rows), :] = jnp.where(lane < hd, o0, o1).astype(o_ref.dtype)
        return carry

    lax.fori_loop(0, nb // MOBA_QBLOCKS, q_rows, 0)


def _moba_prompt(qp, kp, vp, hd):
    n, heads, t, _ = qp.shape
    rows = MOBA_QBLOCKS * MOBA_BLOCK
    assert t % rows == 0 and MOBA_BLOCK % LANES == 0
    spec = pl.BlockSpec((1, 2, t, LANES), lambda b, p: (b, p, 0, 0))
    return pl.pallas_call(
        functools.partial(_moba_kernel, hd=hd),
        grid=(n, heads // 2),
        in_specs=[spec, spec, spec],
        out_specs=pl.BlockSpec((1, t, LANES), lambda b, p: (b, 0, p)),
        out_shape=jax.ShapeDtypeStruct((n, t, heads * hd), BF16),
        scratch_shapes=[pltpu.VMEM((2, LANES, LANES), BF16), pltpu.VMEM((2, LANES, LANES), BF16),
                        pltpu.VMEM((2, rows, LANES), BF16),
                        pltpu.VMEM((2, rows, LANES), F32), pltpu.VMEM((2, rows, LANES), F32)],
        compiler_params=_cparams(("parallel", "parallel"), 48),
        name="moba_prompt",
    )(qp, kp, vp)


SGATE_PAGES = 16


def _sgate_kernel(pt_ref, q_ref, *refs, ppb):
    pages = refs[:SGATE_PAGES]
    idx_ref = refs[SGATE_PAGES]
    gate_ref = refs[SGATE_PAGES + 1]
    s = pl.program_id(1)
    bps = SGATE_PAGES // ppb
    qb = jnp.broadcast_to(q_ref[0], pages[0].shape)
    rows = []
    for b in range(bps):
        acc = pages[b * ppb][...]
        for c in range(1, ppb):
            acc = acc + pages[b * ppb + c][...]
        per_row = jnp.sum(acc * qb, axis=1)
        rows.append(jnp.sum(per_row, axis=1, keepdims=True) / MOBA_BLOCK)
    gate_ref[s] = jnp.concatenate(rows, axis=1)

    @pl.when(s == pl.num_programs(1) - 1)
    def _():
        nsteps = gate_ref.shape[0]
        g = jnp.concatenate([gate_ref[i] for i in range(nsteps)], axis=1)
        col = lax.broadcasted_iota(jnp.int32, g.shape, 1)
        picks = []
        for _ in range(MOBA_TOP_K):
            mx = jnp.max(g, axis=-1, keepdims=True)
            first = jnp.min(jnp.where(g == mx, col, g.shape[1]), axis=-1, keepdims=True)
            picks.append(first)
            g = jnp.where(col == first, NEG, g)
        picks.append(jnp.zeros((g.shape[0], LANES - MOBA_TOP_K), jnp.int32))
        idx_ref[0] = jnp.concatenate(picks, axis=1)


def _sample_gate(q4, cache_kt, layer, page_table):
    n, heads, hd, _ = q4.shape
    page = cache_kt.shape[4]
    n_pages = page_table.shape[1]
    ppb = MOBA_BLOCK // page
    assert n_pages % SGATE_PAGES == 0 and SGATE_PAGES % ppb == 0 and n_pages // ppb >= MOBA_TOP_K
    steps = n_pages // SGATE_PAGES

    def page_spec(c):
        return pl.BlockSpec((None, None, heads, hd, page),
                            lambda b, s, pt: (layer, pt[b * n_pages + s * SGATE_PAGES + c], 0, 0, 0))

    grid_spec = pltpu.PrefetchScalarGridSpec(
        num_scalar_prefetch=1,
        grid=(n, steps),
        in_specs=[pl.BlockSpec((1, heads, hd, 1), lambda b, s, pt: (b, 0, 0, 0))]
                 + [page_spec(c) for c in range(SGATE_PAGES)],
        out_specs=pl.BlockSpec((1, heads, LANES), lambda b, s, pt: (b, 0, 0)),
        scratch_shapes=[pltpu.VMEM((steps, heads, SGATE_PAGES // ppb), F32)],
    )
    return pl.pallas_call(
        functools.partial(_sgate_kernel, ppb=ppb),
        grid_spec=grid_spec,
        out_shape=jax.ShapeDtypeStruct((n, heads, LANES), jnp.int32),
        compiler_params=_cparams(("parallel", "arbitrary"), 40),
        name="sample_gate",
    )(page_table.reshape(-1), q4, *([cache_kt] * SGATE_PAGES))


SATTN_HEADS = 4


def _sattn_kernel(pt_ref, ix_ref, q_ref, kn_ref, vn_ref, *refs, npg):
    o_ref = refs[2 * npg * SATTN_HEADS]
    for hh in range(SATTN_HEADS):
        kp = refs[hh * npg:(hh + 1) * npg]
        vp = refs[(SATTN_HEADS + hh) * npg:(SATTN_HEADS + hh + 1) * npg]
        q = q_ref[0, hh]
        s_new = jnp.sum(q * kn_ref[0, hh], axis=0, keepdims=True)
        logits = [jnp.sum(kp[j][...] * q, axis=0, keepdims=True) for j in range(npg)]
        mx = s_new
        for lg in logits:
            mx = jnp.maximum(mx, jnp.max(lg, axis=1, keepdims=True))
        p_new = jnp.exp(s_new - mx)
        den = p_new
        num = p_new * vn_ref[0, hh]
        for j in range(npg):
            p = jnp.exp(logits[j] - mx)
            den = den + jnp.sum(p, axis=1, keepdims=True)
            num = num + jnp.sum(vp[j][...] * p, axis=1, keepdims=True)
        o_ref[0, hh] = num / den


def _sample_attention(q4, k4, v4, cache_kt, cache_vt, layer, page_table, idx):
    n, heads, hd, _ = q4.shape
    page = cache_kt.shape[4]
    n_pages = page_table.shape[1]
    ppb = MOBA_BLOCK // page
    npg = MOBA_TOP_K * ppb

    hs = SATTN_HEADS
    assert heads % hs == 0

    def page_spec(hh, j):
        r, c = divmod(j, ppb)

        def index(b, g, pt, ix):
            h = g * hs + hh
            return (layer, pt[b * n_pages + ix[(b * heads + h) * MOBA_TOP_K + r] * ppb + c], h, 0, 0)

        return pl.BlockSpec((None, None, None, hd, page), index)

    pages = [page_spec(hh, j) for hh in range(hs) for j in range(npg)]
    tok = pl.BlockSpec((1, hs, hd, 1), lambda b, g, pt, ix: (b, g, 0, 0))
    grid_spec = pltpu.PrefetchScalarGridSpec(
        num_scalar_prefetch=2,
        grid=(n, heads // hs),
        in_specs=[tok, tok, tok] + pages * 2,
        out_specs=tok,
    )
    return pl.pallas_call(
        functools.partial(_sattn_kernel, npg=npg),
        grid_spec=grid_spec,
        out_shape=jax.ShapeDtypeStruct((n, heads, hd, 1), F32),
        compiler_params=_cparams(("parallel", "parallel"), 16),
        name="sample_attention",
    )(page_table.reshape(-1), idx[:, :, :MOBA_TOP_K].reshape(-1), q4, k4, v4,
      *([cache_kt] * (npg * hs)), *([cache_vt] * (npg * hs)))


MOE_ROW_TILE = 512
MOE_TOKEN_TILE = 512
PLAN_ROWS = 32


def _first_max_rows(v, row):
    mx = jnp.max(v, axis=0, keepdims=True)
    first = jnp.min(jnp.where(v == mx, row, v.shape[0]), axis=0, keepdims=True)
    return mx, row == first


def _pack_bf16_pairs(a):
    c = a.shape[1] // 2
    bits = lax.bitcast_convert_type(a.astype(BF16).astype(F32), jnp.uint32)
    return (bits[:, :c] >> 16) | (bits[:, c:] & jnp.uint32(0xFFFF0000))


def _unpack_pairs_f32(u):
    lo = lax.bitcast_convert_type(u << 16, F32)
    hi = lax.bitcast_convert_type(u & jnp.uint32(0xFFFF0000), F32)
    return lo, hi


def _unpack_bf16_pairs(u):
    return jnp.concatenate(_unpack_pairs_f32(u), axis=1).astype(BF16)


def _swiglu(x, wg, wu, wd):
    hg = _dot(x, wg)
    return _dot(((hg * _sigmoid(hg)) * _dot(x, wu)).astype(BF16), wd)


def _outproj_kernel(x_ref, att_ref, gm_ref, wo_ref, g1_ref, b1_ref, wrh_ref, wrl_ref, rb_ref,
                    h_ref, hp_ref, gates_ref, plan_ref, cnt_ref, base_ref, *, alpha):
    aw = att_ref.shape[1]
    mix = _dot(att_ref[...], wo_ref[0:aw, :]) + _dot(gm_ref[...], wo_ref[aw:, :])
    h = _layer_norm_rows(alpha * x_ref[...] + mix, g1_ref[...], b1_ref[...])
    h_ref[...] = h
    hp_ref[...] = _pack_bf16_pairs(h)
    hh = h.astype(BF16)
    hl = (h - hh.astype(F32)).astype(BF16)
    logits = _dot_nt(wrh_ref[...], hh) + _dot_nt(wrh_ref[...], hl) + _dot_nt(wrl_ref[...], hh)
    scores = _sigmoid(logits)
    biased = scores + rb_ref[...]
    ne, tm = biased.shape
    per = ne // N_EXPERT_GROUPS
    sub = lax.broadcasted_iota(jnp.int32, (per, tm), 0)
    gscore = []
    for g in range(N_EXPERT_GROUPS):
        bg = biased[g * per:(g + 1) * per, :]
        m1, at1 = _first_max_rows(bg, sub)
        m2 = jnp.max(jnp.where(at1, -jnp.inf, bg), axis=0, keepdims=True)
        gscore.append(m1 + m2)
    allowed = []
    for g in range(N_EXPERT_GROUPS):
        rank = jnp.zeros((1, tm), jnp.int32)
        for o in range(N_EXPERT_GROUPS):
            if o < g:
                rank = rank + jnp.where(gscore[o] >= gscore[g], 1, 0)
            elif o > g:
                rank = rank + jnp.where(gscore[o] > gscore[g], 1, 0)
        allowed.append(jnp.where(rank < TOPK_GROUPS, biased[g * per:(g + 1) * per, :], NEG))
    v = jnp.concatenate(allowed, axis=0)
    row = lax.broadcasted_iota(jnp.int32, (ne, tm), 0)
    chosen = jnp.zeros((ne, tm), jnp.bool_)
    for _ in range(MOE_TOP_K):
        _, at = _first_max_rows(v, row)
        chosen = chosen | at
        v = jnp.where(at, -jnp.inf, v)
    w = jnp.where(chosen, scores, 0.0)
    w = w / jnp.sum(w, axis=0, keepdims=True) * ROUTED_SCALE
    gates_ref[...] = jnp.concatenate([w, jnp.zeros((LANES - ne, tm), F32)], axis=0).T

    @pl.when(pl.program_id(0) == 0)
    def _():
        base_ref[...] = jnp.zeros(base_ref.shape, F32)

    cf = jnp.where(chosen, 1.0, 0.0)
    cb = cf.astype(BF16)
    before = jnp.where(lax.broadcasted_iota(jnp.int32, (tm, tm), 0) < lax.broadcasted_iota(jnp.int32, (tm, tm), 1),
                       1.0, 0.0).astype(BF16)
    base = base_ref[...]
    pos = base[:, 0:1] + _dot(cb, before)
    base_ref[...] = base + jnp.sum(cf, axis=1, keepdims=True)
    cnt_ref[...] = base_ref[...]
    lower = jnp.where(lax.broadcasted_iota(jnp.int32, (ne, ne), 1) < lax.broadcasted_iota(jnp.int32, (ne, ne), 0),
                      1.0, 0.0).astype(BF16)
    slot = _dot(lower, cb)
    erow = row.astype(F32)
    prow = lax.broadcasted_iota(jnp.int32, (PLAN_ROWS, tm), 0)
    plan = jnp.zeros((PLAN_ROWS, tm), F32)
    for k in range(MOE_TOP_K):
        sel = chosen & (slot == k)
        for j, val in enumerate((erow, pos, w)):
            picked = jnp.sum(jnp.where(sel, val, 0.0), axis=0, keepdims=True)
            plan = jnp.where(prow == j * MOE_TOP_K + k, picked, plan)
    plan_ref[...] = jnp.concatenate([plan, jnp.zeros((LANES - PLAN_ROWS, tm), F32)], axis=0).T


def _split_bf16(a):
    hi = a.astype(BF16)
    return hi, (a - hi.astype(F32)).astype(BF16)


def _outproj(x, att, gm, w_o, g1, b1, w_router, router_bias, alpha, tm):
    m, dm = x.shape
    ne = w_router.shape[1]
    assert m % tm == 0 and ne <= LANES and ne % N_EXPERT_GROUPS == 0 and 3 * MOE_TOP_K <= PLAN_ROWS
    assert m < 1 << 24
    wrh, wrl = _split_bf16(w_router.T)
    row = lambda c: pl.BlockSpec((tm, c), lambda i: (i, 0))
    full = lambda a: pl.BlockSpec(a.shape, lambda i: (0,) * a.ndim)
    args = (x, att, gm, w_o, g1.reshape(1, dm), b1.reshape(1, dm), wrh, wrl, router_bias.reshape(ne, 1))
    return pl.pallas_call(
        functools.partial(_outproj_kernel, alpha=alpha),
        grid=(m // tm,),
        in_specs=[row(dm), row(att.shape[1]), row(gm.shape[1])] + [full(a) for a in args[3:]],
        out_specs=[row(dm), row(dm // 2), row(LANES), row(LANES), pl.BlockSpec((ne, LANES), lambda i: (0, 0))],
        out_shape=[jax.ShapeDtypeStruct((m, dm), F32), jax.ShapeDtypeStruct((m, dm // 2), jnp.uint32),
                   jax.ShapeDtypeStruct((m, LANES), F32), jax.ShapeDtypeStruct((m, LANES), F32),
                   jax.ShapeDtypeStruct((ne, LANES), F32)],
        scratch_shapes=[pltpu.VMEM((ne, LANES), F32)],
        compiler_params=_cparams(("arbitrary",), 40),
        name="outproj_router",
    )(*args)


def _with_rows(dest_hbm, bufs, sem, step, nsteps, per_step, body):
    def copy(st, sl):
        return pltpu.make_async_copy(dest_hbm.at[pl.ds(st * per_step, per_step)], bufs[sl], sem.at[sl])

    @pl.when(step == 0)
    def _():
        copy(0, 0).start()

    for sl in range(2):
        @pl.when(step % 2 == sl)
        def _():
            @pl.when(step + 1 < nsteps)
            def _():
                copy(step + 1, 1 - sl).start()

            copy(step, sl).wait()
            body(bufs[sl])


def _dispatch_kernel(off_ref, cnt_ref, dest_hbm, hp_ref, xs_hbm, rows0, rows1, zero_ref, sem_dest, sem_row, *, tr):
    step, nsteps = pl.program_id(0), pl.num_programs(0)
    td = hp_ref.shape[0]

    def row_copy(t, d):
        return pltpu.make_async_copy(hp_ref.at[pl.ds(t, 1)], xs_hbm.at[pl.ds(d, 1)], sem_row)

    def scatter(rows):
        def issue(t, c):
            for k in range(MOE_TOP_K):
                row_copy(t, rows[t * MOE_TOP_K + k]).start(priority=k % 2)
            return c

        lax.fori_loop(0, td, issue, 0)

    def drain(t, c):
        for k in range(MOE_TOP_K):
            row_copy(0, 0).wait()
        return c

    _with_rows(dest_hbm, (rows0, rows1), sem_dest, step, nsteps, td * MOE_TOP_K, scatter)
    lax.fori_loop(0, td, drain, 0)

    @pl.when(step == nsteps - 1)
    def _():
        zero_ref[...] = jnp.zeros(zero_ref.shape, zero_ref.dtype)

        def zero_copy(d):
            return pltpu.make_async_copy(zero_ref.at[pl.ds(0, 1)], xs_hbm.at[pl.ds(d, 1)], sem_row)

        def per_expert(e, c):
            lo = off_ref[e] + cnt_ref[e]
            hi = off_ref[e] + (cnt_ref[e] + tr - 1) // tr * tr
            lax.fori_loop(lo, hi, lambda d, c2: (zero_copy(d).start(), c2)[1], 0)
            lax.fori_loop(lo, hi, lambda d, c2: (zero_copy(0).wait(), c2)[1], 0)
            return c

        lax.fori_loop(0, cnt_ref.shape[0], per_expert, 0)


def _expert_kernel(te_ref, nu_ref, x_ref, wg_ref, wu_ref, wd_ref, y_ref):
    @pl.when(pl.program_id(0) < nu_ref[0])
    def _():
        x = _unpack_bf16_pairs(x_ref[...])
        y_ref[...] = _pack_bf16_pairs(_swiglu(x, wg_ref[...], wu_ref[...], wd_ref[...]))


def _combine_kernel(dest_hbm, h_ref, hp_ref, plan_ref, ys_hbm, wsg_ref, wsu_ref, wsd_ref, g2_ref, b2_ref,
                    o_ref, rows0, rows1, ybuf, sem_dest, sem_row, *, alpha):
    step, nsteps = pl.program_id(0), pl.num_programs(0)
    tc = h_ref.shape[0]
    half = hp_ref.shape[1]

    def row_copy(t, k, d):
        return pltpu.make_async_copy(ys_hbm.at[pl.ds(d, 1)], ybuf.at[pl.ds(t, 1), pl.ds(k * half, half)], sem_row)

    def gather(rows):
        def issue(t, c):
            for k in range(MOE_TOP_K):
                row_copy(t, k, rows[t * MOE_TOP_K + k]).start(priority=k % 2)
            return c

        lax.fori_loop(0, tc, issue, 0)

    def drain(t, c):
        for k in range(MOE_TOP_K):
            row_copy(0, k, 0).wait()
        return c

    _with_rows(dest_hbm, (rows0, rows1), sem_dest, step, nsteps, tc * MOE_TOP_K, gather)
    shared = _swiglu(_unpack_bf16_pairs(hp_ref[...]), wsg_ref[...], wsu_ref[...], wsd_ref[...])
    lax.fori_loop(0, tc, drain, 0)

    plan = plan_ref[...]
    lo_acc = jnp.zeros((tc, half), F32)
    hi_acc = jnp.zeros((tc, half), F32)
    for k in range(MOE_TOP_K):
        gate = plan[:, 2 * MOE_TOP_K + k:2 * MOE_TOP_K + k + 1]
        lo, hi = _unpack_pairs_f32(ybuf[:, k * half:(k + 1) * half])
        lo_acc = lo_acc + gate * lo
        hi_acc = hi_acc + gate * hi
    ffn = jnp.concatenate([lo_acc, hi_acc], axis=1) + shared
    o_ref[...] = _layer_norm_rows(alpha * h_ref[...] + ffn, g2_ref[...], b2_ref[...])


def _moe_sparse(h, hp, plan, cnt, wg, wu, wd, wsg, wsu, wsd, g2, b2, alpha):
    m, dm = h.shape
    ne, _, ed = wg.shape
    tr, tt = MOE_ROW_TILE, MOE_TOKEN_TILE
    assert m % tt == 0 and (tt * MOE_TOP_K) % 1024 == 0
    n_tiles = (m * MOE_TOP_K) // tr + ne
    rows = n_tiles * tr

    counts = cnt[:, 0].astype(jnp.int32)
    padded = (counts + tr - 1) // tr * tr
    ends = jnp.cumsum(padded)
    off = ends - padded
    n_used = (ends[-1:] // tr).astype(jnp.int32)
    tile_start = jnp.arange(n_tiles, dtype=jnp.int32) * tr
    tile_e = jnp.minimum(jnp.sum((ends[None, :] <= tile_start[:, None]).astype(jnp.int32), axis=1), ne - 1)
    plan_t = plan[:, 0:2 * MOE_TOP_K].T.astype(jnp.int32)
    dest = (off[plan_t[0:MOE_TOP_K]] + plan_t[MOE_TOP_K:]).T.reshape(-1)

    row_scratch = [pltpu.SMEM((tt * MOE_TOP_K,), jnp.int32)] * 2
    sems = [pltpu.SemaphoreType.DMA((2,)), pltpu.SemaphoreType.DMA(())]
    any_spec = pl.BlockSpec(memory_space=pl.ANY)

    xs = pl.pallas_call(
        functools.partial(_dispatch_kernel, tr=tr),
        grid_spec=pltpu.PrefetchScalarGridSpec(
            num_scalar_prefetch=2, grid=(m // tt,),
            in_specs=[any_spec, pl.BlockSpec((tt, dm // 2), lambda i, *_: (i, 0))],
            out_specs=any_spec,
            scratch_shapes=row_scratch + [pltpu.VMEM((8, dm // 2), jnp.uint32)] + sems),
        out_shape=jax.ShapeDtypeStruct((rows, dm // 2), jnp.uint32),
        compiler_params=_cparams(("arbitrary",), 16),
        name="moe_dispatch",
    )(off, counts, dest, hp)

    used = lambda i, te, nu: (jnp.minimum(i, nu[0] - 1), 0)
    ys = pl.pallas_call(
        _expert_kernel,
        grid_spec=pltpu.PrefetchScalarGridSpec(
            num_scalar_prefetch=2, grid=(n_tiles,),
            in_specs=[pl.BlockSpec((tr, dm // 2), used),
                      pl.BlockSpec((None, dm, ed), lambda i, te, nu: (te[i], 0, 0)),
                      pl.BlockSpec((None, dm, ed), lambda i, te, nu: (te[i], 0, 0)),
                      pl.BlockSpec((None, ed, dm), lambda i, te, nu: (te[i], 0, 0))],
            out_specs=pl.BlockSpec((tr, dm // 2), used)),
        out_shape=jax.ShapeDtypeStruct((rows, dm // 2), jnp.uint32),
        compiler_params=_cparams(("arbitrary",), 32),
        name="moe_experts",
    )(tile_e, n_used, xs, wg, wu, wd)

    row = lambda c: pl.BlockSpec((tt, c), lambda i: (i, 0))
    full = lambda a: pl.BlockSpec(a.shape, lambda i: (0,) * a.ndim)
    g2r, b2r = g2.reshape(1, dm), b2.reshape(1, dm)
    return pl.pallas_call(
        functools.partial(_combine_kernel, alpha=alpha),
        grid=(m // tt,),
        in_specs=[any_spec, row(dm), row(dm // 2), row(LANES), any_spec,
                  full(wsg), full(wsu), full(wsd), full(g2r), full(b2r)],
        out_specs=row(dm),
        scratch_shapes=row_scratch + [pltpu.VMEM((tt, MOE_TOP_K * dm // 2), jnp.uint32)] + sems,
        out_shape=jax.ShapeDtypeStruct((m, dm), F32),
        compiler_params=_cparams(("arbitrary",), 32),
        name="moe_combine",
    )(dest, h, hp, plan, ys, wsg, wsu, wsd, g2r, b2r)


def _moe_dense_kernel(h_ref, hp_ref, gates_ref, wg_ref, wu_ref, wd_ref, wsg_ref, wsu_ref, wsd_ref, g2_ref, b2_ref,
                      o_ref, acc_ref, *, alpha):
    e = pl.program_id(0)
    x = _unpack_bf16_pairs(hp_ref[...])
    hg = _dot(x, wg_ref[...])
    hu = _dot(x, wu_ref[...])
    gates = gates_ref[...]
    lane = lax.broadcasted_iota(jnp.int32, gates.shape, 1)
    gcol = jnp.sum(jnp.where(lane == e, gates, 0.0), axis=-1, keepdims=True)
    y = _dot(((hg * _sigmoid(hg)) * hu * gcol).astype(BF16), wd_ref[...])

    @pl.when(e == 0)
    def _():
        acc_ref[...] = y

    @pl.when(e > 0)
    def _():
        acc_ref[...] += y

    @pl.when(e == pl.num_programs(0) - 1)
    def _():
        ffn = acc_ref[...] + _swiglu(x, wsg_ref[...], wsu_ref[...], wsd_ref[...])
        o_ref[...] = _layer_norm_rows(alpha * h_ref[...] + ffn, g2_ref[...], b2_ref[...])


def _moe_dense(h, hp, gates, wg, wu, wd, wsg, wsu, wsd, g2, b2, alpha):
    m, dm = h.shape
    ne, _, ed = wg.shape
    full = lambda a: pl.BlockSpec(a.shape, lambda e: (0,) * a.ndim)
    g2r, b2r = g2.reshape(1, dm), b2.reshape(1, dm)
    return pl.pallas_call(
        functools.partial(_moe_dense_kernel, alpha=alpha),
        grid=(ne,),
        in_specs=[full(h), full(hp), full(gates),
                  pl.BlockSpec((None, dm, ed), lambda e: (e, 0, 0)),
                  pl.BlockSpec((None, dm, ed), lambda e: (e, 0, 0)),
                  pl.BlockSpec((None, ed, dm), lambda e: (e, 0, 0)),
                  full(wsg), full(wsu), full(wsd), full(g2r), full(b2r)],
        out_specs=full(h),
        out_shape=jax.ShapeDtypeStruct((m, dm), F32),
        scratch_shapes=[pltpu.VMEM((m, dm), F32)],
        compiler_params=_cparams(("arbitrary",), 32),
        name="moe_dense",
    )(h, hp, gates, wg, wu, wd, wsg, wsu, wsd, g2r, b2r)


def kernel(x_prompt, x_sample, cache_k, cache_v, page_table, w_in, w_o, gm_ln_g, gm_ln_b, gm_w_s, gm_b_s,
           ln1_g, ln1_b, w_router, router_bias, w_gate, w_up, w_down, ws_gate, ws_up, ws_down, ln2_g, ln2_b):
    depth = w_in.shape[0]
    heads, page, hd = cache_k.shape[2], cache_k.shape[3], cache_k.shape[4]
    n, t, dm = x_prompt.shape
    ns, ts, _ = x_sample.shape
    assert ts == 1 and MOBA_BLOCK % page == 0 and MOBA_BLOCK == 1 << BLOCK_SHIFT
    alpha = float((2 * depth) ** 0.25)
    xp, xs = x_prompt.reshape(n * t, dm), x_sample.reshape(ns, dm)
    cache_kt, cache_vt = jnp.swapaxes(cache_k, 3, 4), jnp.swapaxes(cache_v, 3, 4)
    outs = [[] for _ in range(5)]
    for l in range(depth):
        w_in_b = w_in[l].astype(BF16)
        w_o_b = w_o[l].astype(BF16)
        experts = tuple(a[l].astype(BF16) for a in (w_gate, w_up, w_down, ws_gate, ws_up, ws_down))
        gm_w = (gm_ln_g[l], gm_ln_b[l], gm_w_s[l], gm_b_s[l])
        route = (w_o_b, ln1_g[l], ln1_b[l], w_router[l], router_bias[l], alpha)
        norm2 = (ln2_g[l], ln2_b[l], alpha)

        qp, kp, vp, k_pages, v_pages, gm = _proj_prompt(xp.reshape(n, t, dm), w_in_b, *gm_w, heads, hd, page)
        att = _moba_prompt(qp, kp, vp, hd)
        h, hp, _, plan, cnt = _outproj(xp, att.reshape(n * t, heads * hd), gm.reshape(n * t, -1), *route, 512)
        xp = _moe_sparse(h, hp, plan, cnt, *experts, *norm2)
        outs[0].append(k_pages)
        outs[1].append(v_pages)

        q_s, k_s, v_s, gm_s, vn_s = _proj_sample(xs, w_in_b, *gm_w, heads, hd)
        col4 = (ns, heads, hd, 1)
        idx = _sample_gate(q_s.reshape(col4), cache_kt, l, page_table)
        att_s = _sample_attention(q_s.reshape(col4), k_s.reshape(col4), v_s.reshape(col4),
                                  cache_kt, cache_vt, l, page_table, idx)
        h, hp, gates, _, _ = _outproj(xs, att_s.reshape(ns, heads * hd).astype(BF16), gm_s, *route, ns)
        xs = _moe_dense(h, hp, gates, *experts, *norm2)
        shp4 = (ns, heads, 1, hd)
        outs[2].append(k_s.reshape(shp4))
        outs[3].append(v_s.reshape(shp4))
        outs[4].append(vn_s.reshape(ns, 1, -1))
    return (xp.reshape(n, t, dm), xs.reshape(ns, 1, dm)) + tuple(jnp.stack(o) for o in outs)
```

```python
import functools

import jax
import jax.numpy as jnp
from jax import lax
from jax.experimental import pallas as pl
from jax.experimental.pallas import tpu as pltpu

F32 = jnp.float32
BF16 = jnp.bfloat16

MOBA_BLOCK = 256
BLOCK_SHIFT = 8
MOBA_TOP_K = 3
MOBA_QBLOCKS = 4
MOE_TOP_K = 8
N_EXPERT_GROUPS = 8
TOPK_GROUPS = 4
ROUTED_SCALE = 2.5
LN_EPS = 1e-5
NEG = -1e30
MASK_BIG = float(2.0 ** 100)
LANES = 128
V7X_VMEM_BYTES = 64 * 1024 * 1024


def _cparams(semantics, vmem_mb):
    assert vmem_mb * 1024 * 1024 < V7X_VMEM_BYTES
    return pltpu.CompilerParams(dimension_semantics=semantics, vmem_limit_bytes=vmem_mb * 1024 * 1024)


def _dot(a, b):
    return jnp.dot(a, b, preferred_element_type=F32)


def _dot_nt(a, b):
    return lax.dot_general(a, b, (((1,), (1,)), ((), ())), preferred_element_type=F32)


def _layer_norm_rows(y, g, b):
    mu = jnp.mean(y, axis=-1, keepdims=True)
    var = jnp.mean(jnp.square(y - mu), axis=-1, keepdims=True)
    return (y - mu) * lax.rsqrt(var + LN_EPS) * g + b


def _sigmoid(x):
    return 1.0 / (1.0 + jnp.exp(-x))


def _gmlp_norm(zgv, lng, lnb, groups, gdim):
    vg = jax.nn.gelu(zgv)
    out = []
    for g in range(groups):
        sl = slice(g * gdim, (g + 1) * gdim)
        out.append(_layer_norm_rows(vg[:, sl], lng[:, sl], lnb[:, sl]))
    return out


def _proj_prompt_kernel(x_ref, w_ref, lng_ref, lnb_ref, ws_ref, bias_ref,
                        q_ref, k_ref, v_ref, ko_ref, vo_ref, gm_ref, *, heads, hd, groups, chunk):
    tm = x_ref.shape[1]
    aw = heads * hd
    gdim = gm_ref.shape[2] // groups
    x = x_ref[0].astype(BF16)
    zq = _dot(x, w_ref[:, 0:aw])
    zk = _dot(x, w_ref[:, aw:2 * aw])
    zv = _dot(x, w_ref[:, 2 * aw:3 * aw])

    lane = lax.broadcasted_iota(jnp.int32, (tm, LANES), 1)
    row = lax.broadcasted_iota(jnp.int32, (tm, LANES), 0)
    blk = (pl.program_id(1) * tm + row) >> BLOCK_SHIFT
    lo = lane < hd
    for p in range(heads // 2):
        sl = slice(p * LANES, (p + 1) * LANES)
        cq = zq[:, sl] * (hd ** -0.5)
        ck = zk[:, sl]
        cv = zv[:, sl]
        q_ref[0, 2 * p] = jnp.where(lo, cq, 0.0).astype(BF16)
        q_ref[0, 2 * p + 1] = jnp.where(lo, 0.0, cq).astype(BF16)
        k_ref[0, 2 * p] = jnp.where(lo, ck, jnp.where(lane - hd == blk, -MASK_BIG, 0.0)).astype(BF16)
        k_ref[0, 2 * p + 1] = jnp.where(lo, jnp.where(lane == blk, -MASK_BIG, 0.0), ck).astype(BF16)
        v_ref[0, 2 * p] = jnp.where(lo, cv, jnp.where(lane == hd, 1.0, 0.0)).astype(BF16)
        v_ref[0, 2 * p + 1] = jnp.where(lo, jnp.where(lane == 0, 1.0, 0.0), cv).astype(BF16)

    page = ko_ref.shape[3]
    for c in range(tm // page):
        for h in range(heads):
            ko_ref[0, c, h] = zk[c * page:(c + 1) * page, h * hd:(h + 1) * hd]
            vo_ref[0, c, h] = zv[c * page:(c + 1) * page, h * hd:(h + 1) * hd]

    gw = groups * gdim
    u = jax.nn.gelu(_dot(x, w_ref[:, 3 * aw:3 * aw + gw]))
    vn = _gmlp_norm(_dot(x, w_ref[:, 3 * aw + gw:3 * aw + 2 * gw]), lng_ref[...], lnb_ref[...], groups, gdim)
    ci = lax.broadcasted_iota(jnp.int32, (chunk, chunk), 0)
    cj = lax.broadcasted_iota(jnp.int32, (chunk, chunk), 1)
    for g in range(groups):
        wm = jnp.where(ci >= cj, ws_ref[g], 0.0).astype(BF16)
        gs = slice(g * gdim, (g + 1) * gdim)
        for c in range(tm // chunk):
            rs = slice(c * chunk, (c + 1) * chunk)
            s = _dot(wm, vn[g][rs].astype(BF16)) + bias_ref[:, gs]
            gm_ref[0, rs, gs] = (u[rs, gs] * s).astype(BF16)


def _proj_sample_kernel(x_ref, w_ref, lng_ref, lnb_ref, w00_ref, b0_ref,
                        q_ref, k_ref, v_ref, gm_ref, vn_ref, *, heads, hd, groups):
    aw = heads * hd
    gw = vn_ref.shape[1]
    gdim = gw // groups
    x = x_ref[...].astype(BF16)
    q_ref[...] = _dot(x, w_ref[:, 0:aw]) * (hd ** -0.5)
    k_ref[...] = _dot(x, w_ref[:, aw:2 * aw])
    v_ref[...] = _dot(x, w_ref[:, 2 * aw:3 * aw])
    u = jax.nn.gelu(_dot(x, w_ref[:, 3 * aw:3 * aw + gw]))
    vn = _gmlp_norm(_dot(x, w_ref[:, 3 * aw + gw:3 * aw + 2 * gw]), lng_ref[...], lnb_ref[...], groups, gdim)
    for g in range(groups):
        gs = slice(g * gdim, (g + 1) * gdim)
        vn_ref[:, gs] = vn[g]
        gm_ref[:, gs] = (u[:, gs] * (w00_ref[:, gs] * vn[g] + b0_ref[:, gs])).astype(BF16)


def _proj_prompt(x, w_in, lng, lnb, w_s, b_s, heads, hd, page):
    n, t, dm = x.shape
    groups, chunk = w_s.shape[0], w_s.shape[1]
    gw = lng.shape[-1]
    gdim = gw // groups
    tm = 512
    assert t % tm == 0 and tm % MOBA_BLOCK == 0 and tm % page == 0 and tm % chunk == 0
    assert heads % 2 == 0 and 2 * hd == LANES and t // MOBA_BLOCK <= hd
    bias = jnp.repeat(b_s.T, gdim, axis=1)
    att_shape = jax.ShapeDtypeStruct((n, heads, t, LANES), BF16)
    page_shape = jax.ShapeDtypeStruct((n, t // page, heads, page, hd), F32)
    att_spec = pl.BlockSpec((1, heads, tm, LANES), lambda b, i: (b, 0, i, 0))
    page_spec = pl.BlockSpec((1, tm // page, heads, page, hd), lambda b, i: (b, i, 0, 0, 0))
    full = lambda a: pl.BlockSpec(a.shape, lambda b, i: (0,) * a.ndim)
    lng2, lnb2 = lng.reshape(1, gw), lnb.reshape(1, gw)
    return pl.pallas_call(
        functools.partial(_proj_prompt_kernel, heads=heads, hd=hd, groups=groups, chunk=chunk),
        grid=(n, t // tm),
        in_specs=[pl.BlockSpec((1, tm, dm), lambda b, i: (b, i, 0)), full(w_in), full(lng2), full(lnb2),
                  full(w_s), full(bias)],
        out_specs=[att_spec, att_spec, att_spec, page_spec, page_spec,
                   pl.BlockSpec((1, tm, gw), lambda b, i: (b, i, 0))],
        out_shape=[att_shape, att_shape, att_shape, page_shape, page_shape,
                   jax.ShapeDtypeStruct((n, t, gw), BF16)],
        compiler_params=_cparams(("parallel", "parallel"), 48),
        name="proj_prompt",
    )(x, w_in, lng2, lnb2, w_s, bias)


def _proj_sample(x, w_in, lng, lnb, w_s, b_s, heads, hd):
    m, dm = x.shape
    groups = w_s.shape[0]
    gw = lng.shape[-1]
    gdim = gw // groups
    aw = heads * hd
    w00 = jnp.repeat(w_s[:, 0, 0], gdim).reshape(1, gw)
    b0 = jnp.repeat(b_s[:, 0], gdim).reshape(1, gw)
    f = lambda c, dt: jax.ShapeDtypeStruct((m, c), dt)
    return pl.pallas_call(
        functools.partial(_proj_sample_kernel, heads=heads, hd=hd, groups=groups),
        out_shape=[f(aw, F32), f(aw, F32), f(aw, F32), f(gw, BF16), f(gw, F32)],
        compiler_params=_cparams(None, 32),
        name="proj_sample",
    )(x, w_in, lng.reshape(1, gw), lnb.reshape(1, gw), w00, b0)


def _moba_kernel(q_ref, k_ref, v_ref, o_ref, kmh_ref, kml_ref, q2_ref, m_ref, acc_ref, *, hd):
    t = q_ref.shape[2]
    bq = MOBA_BLOCK
    nb = t // bq
    rows = MOBA_QBLOCKS * bq
    nbp = -(-nb // 8) * 8
    aux0 = (hd, 0)

    key_blk = lax.broadcasted_iota(jnp.int32, (LANES, t), 1) >> BLOCK_SHIFT
    out_row = lax.broadcasted_iota(jnp.int32, (LANES, t), 0)
    lane_sq = lax.broadcasted_iota(jnp.int32, (LANES, LANES), 1)
    for hh in range(2):
        avg = jnp.where(out_row - aux0[hh] == key_blk, 1.0 / bq, 0.0).astype(BF16)
        km_rows = _dot(avg, k_ref[0, hh])
        km_rows = jnp.where(lane_sq < hd if hh == 0 else lane_sq >= hd, km_rows, 0.0)
        hi = km_rows.astype(BF16)
        kmh_ref[hh] = hi
        kml_ref[hh] = (km_rows - hi.astype(F32)).astype(BF16)

    lane = lax.broadcasted_iota(jnp.int32, (rows, LANES), 1)
    g_blk = lax.broadcasted_iota(jnp.int32, (nbp, rows), 0)
    g_qblk = lax.broadcasted_iota(jnp.int32, (nbp, rows), 1) >> BLOCK_SHIFT

    def update(hh, j, r_lo, causal):
        c0 = pl.multiple_of(j * bq, bq)
        s = _dot_nt(q2_ref[hh, r_lo:, :], k_ref[0, hh, pl.ds(c0, bq), :])
        if causal:
            rr = lax.broadcasted_iota(jnp.int32, s.shape, 0)
            cc = lax.broadcasted_iota(jnp.int32, s.shape, 1)
            s = jnp.where(cc <= rr, s, NEG)
        m_prev = m_ref[hh, r_lo:, :]
        m_new = jnp.maximum(m_prev, jnp.max(s, axis=-1, keepdims=True))
        alpha = jnp.exp(m_prev - m_new)
        p = jnp.exp(s - jnp.concatenate([m_new] * (bq // LANES), axis=1))
        m_ref[hh, r_lo:, :] = m_new
        acc_ref[hh, r_lo:, :] = alpha * acc_ref[hh, r_lo:, :] + _dot(p.astype(BF16), v_ref[0, hh, pl.ds(c0, bq), :])

    def q_rows(sb, carry):
        r0 = pl.multiple_of(sb * rows, rows)
        b0 = sb * MOBA_QBLOCKS
        for hh in range(2):
            q = q_ref[0, hh, pl.ds(r0, rows), :]
            gate = _dot_nt(kmh_ref[hh], q) + _dot_nt(kml_ref[hh], q)
            past = g_blk < b0 + g_qblk
            g = jnp.where(past, gate[aux0[hh]:aux0[hh] + nbp, :], NEG)
            notsel = jnp.where(past, 1.0, 0.0)
            for _ in range(MOBA_TOP_K):
                mx, at = _first_max_rows(g, g_blk)
                pick = at & (mx > 0.5 * NEG)
                notsel = jnp.where(pick, 0.0, notsel)
                g = jnp.where(pick, NEG, g)
            pieces = [jnp.zeros((aux0[hh], rows), F32)] if aux0[hh] else []
            pieces += [notsel, jnp.zeros((LANES - aux0[hh] - nbp, rows), F32)]
            q2_ref[hh] = q + jnp.concatenate(pieces, axis=0).T.astype(BF16)
            m_ref[hh] = jnp.full((rows, LANES), NEG, F32)
            acc_ref[hh] = jnp.zeros((rows, LANES), F32)

        def past_blocks(g, c):
            for d in range(MOBA_QBLOCKS):
                for hh in range(2):
                    update(hh, g * MOBA_QBLOCKS + d, 0, False)
            return c

        lax.fori_loop(0, sb, past_blocks, 0)
        for d in range(MOBA_QBLOCKS):
            for hh in range(2):
                update(hh, b0 + d, d * bq, True)
        a0 = acc_ref[0]
        a1 = acc_ref[1]
        o0 = a0 * (1.0 / a0[:, hd:hd + 1])
        o1 = a1 * (1.0 / a1[:, 0:1])
        o_ref[0, pl.ds(r0, rows), :] = jnp.where(lane < hd, o0, o1).astype(o_ref.dtype)
        return carry

    lax.fori_loop(0, nb // MOBA_QBLOCKS, q_rows, 0)


def _moba_prompt(qp, kp, vp, hd):
    n, heads, t, _ = qp.shape
    rows = MOBA_QBLOCKS * MOBA_BLOCK
    assert t % rows == 0 and MOBA_BLOCK % LANES == 0
    spec = pl.BlockSpec((1, 2, t, LANES), lambda b, p: (b, p, 0, 0))
    return pl.pallas_call(
        functools.partial(_moba_kernel, hd=hd),
        grid=(n, heads // 2),
        in_specs=[spec, spec, spec],
        out_specs=pl.BlockSpec((1, t, LANES), lambda b, p: (b, 0, p)),
        out_shape=jax.ShapeDtypeStruct((n, t, heads * hd), BF16),
        scratch_shapes=[pltpu.VMEM((2, LANES, LANES), BF16), pltpu.VMEM((2, LANES, LANES), BF16),
                        pltpu.VMEM((2, rows, LANES), BF16),
                        pltpu.VMEM((2, rows, LANES), F32), pltpu.VMEM((2, rows, LANES), F32)],
        compiler_params=_cparams(("parallel", "parallel"), 48),
        name="moba_prompt",
    )(qp, kp, vp)


SGATE_PAGES = 16


def _sgate_kernel(pt_ref, q_ref, *refs, ppb):
    pages = refs[:SGATE_PAGES]
    idx_ref = refs[SGATE_PAGES]
    gate_ref = refs[SGATE_PAGES + 1]
    s = pl.program_id(1)
    bps = SGATE_PAGES // ppb
    qb = jnp.broadcast_to(q_ref[0], pages[0].shape)
    rows = []
    for b in range(bps):
        acc = pages[b * ppb][...]
        for c in range(1, ppb):
            acc = acc + pages[b * ppb + c][...]
        per_row = jnp.sum(acc * qb, axis=1)
        rows.append(jnp.sum(per_row, axis=1, keepdims=True) / MOBA_BLOCK)
    gate_ref[s] = jnp.concatenate(rows, axis=1)

    @pl.when(s == pl.num_programs(1) - 1)
    def _():
        nsteps = gate_ref.shape[0]
        g = jnp.concatenate([gate_ref[i] for i in range(nsteps)], axis=1)
        col = lax.broadcasted_iota(jnp.int32, g.shape, 1)
        picks = []
        for _ in range(MOBA_TOP_K):
            mx = jnp.max(g, axis=-1, keepdims=True)
            first = jnp.min(jnp.where(g == mx, col, g.shape[1]), axis=-1, keepdims=True)
            picks.append(first)
            g = jnp.where(col == first, NEG, g)
        picks.append(jnp.zeros((g.shape[0], LANES - MOBA_TOP_K), jnp.int32))
        idx_ref[0] = jnp.concatenate(picks, axis=1)


def _sample_gate(q4, cache_kt, layer, page_table):
    n, heads, hd, _ = q4.shape
    page = cache_kt.shape[4]
    n_pages = page_table.shape[1]
    ppb = MOBA_BLOCK // page
    assert n_pages % SGATE_PAGES == 0 and SGATE_PAGES % ppb == 0 and n_pages // ppb >= MOBA_TOP_K
    steps = n_pages // SGATE_PAGES

    def page_spec(c):
        return pl.BlockSpec((None, None, heads, hd, page),
                            lambda b, s, pt: (layer, pt[b * n_pages + s * SGATE_PAGES + c], 0, 0, 0))

    grid_spec = pltpu.PrefetchScalarGridSpec(
        num_scalar_prefetch=1,
        grid=(n, steps),
        in_specs=[pl.BlockSpec((1, heads, hd, 1), lambda b, s, pt: (b, 0, 0, 0))]
                 + [page_spec(c) for c in range(SGATE_PAGES)],
        out_specs=pl.BlockSpec((1, heads, LANES), lambda b, s, pt: (b, 0, 0)),
        scratch_shapes=[pltpu.VMEM((steps, heads, SGATE_PAGES // ppb), F32)],
    )
    return pl.pallas_call(
        functools.partial(_sgate_kernel, ppb=ppb),
        grid_spec=grid_spec,
        out_shape=jax.ShapeDtypeStruct((n, heads, LANES), jnp.int32),
        compiler_params=_cparams(("parallel", "arbitrary"), 40),
        name="sample_gate",
    )(page_table.reshape(-1), q4, *([cache_kt] * SGATE_PAGES))


SATTN_HEADS = 4


def _sattn_kernel(pt_ref, ix_ref, q_ref, kn_ref, vn_ref, *refs, npg):
    o_ref = refs[2 * npg * SATTN_HEADS]
    for hh in range(SATTN_HEADS):
        kp = refs[hh * npg:(hh + 1) * npg]
        vp = refs[(SATTN_HEADS + hh) * npg:(SATTN_HEADS + hh + 1) * npg]
        q = q_ref[0, hh]
        s_new = jnp.sum(q * kn_ref[0, hh], axis=0, keepdims=True)
        logits = [jnp.sum(kp[j][...] * q, axis=0, keepdims=True) for j in range(npg)]
        mx = s_new
        for lg in logits:
            mx = jnp.maximum(mx, jnp.max(lg, axis=1, keepdims=True))
        p_new = jnp.exp(s_new - mx)
        den = p_new
        num = p_new * vn_ref[0, hh]
        for j in range(npg):
            p = jnp.exp(logits[j] - mx)
            den = den + jnp.sum(p, axis=1, keepdims=True)
            num = num + jnp.sum(vp[j][...] * p, axis=1, keepdims=True)
        o_ref[0, hh] = num / den


def _sample_attention(q4, k4, v4, cache_kt, cache_vt, layer, page_table, idx):
    n, heads, hd, _ = q4.shape
    page = cache_kt.shape[4]
    n_pages = page_table.shape[1]
    ppb = MOBA_BLOCK // page
    npg = MOBA_TOP_K * ppb

    hs = SATTN_HEADS
    assert heads % hs == 0

    def page_spec(hh, j):
        r, c = divmod(j, ppb)

        def index(b, g, pt, ix):
            h = g * hs + hh
            return (layer, pt[b * n_pages + ix[(b * heads + h) * MOBA_TOP_K + r] * ppb + c], h, 0, 0)

        return pl.BlockSpec((None, None, None, hd, page), index)

    pages = [page_spec(hh, j) for hh in range(hs) for j in range(npg)]
    tok = pl.BlockSpec((1, hs, hd, 1), lambda b, g, pt, ix: (b, g, 0, 0))
    grid_spec = pltpu.PrefetchScalarGridSpec(
        num_scalar_prefetch=2,
        grid=(n, heads // hs),
        in_specs=[tok, tok, tok] + pages * 2,
        out_specs=tok,
    )
    return pl.pallas_call(
        functools.partial(_sattn_kernel, npg=npg),
        grid_spec=grid_spec,
        out_shape=jax.ShapeDtypeStruct((n, heads, hd, 1), F32),
        compiler_params=_cparams(("parallel", "parallel"), 16),
        name="sample_attention",
    )(page_table.reshape(-1), idx[:, :, :MOBA_TOP_K].reshape(-1), q4, k4, v4,
      *([cache_kt] * (npg * hs)), *([cache_vt] * (npg * hs)))


MOE_ROW_TILE = 512
MOE_TOKEN_TILE = 512
PLAN_ROWS = 32


def _first_max_rows(v, row):
    mx = jnp.max(v, axis=0, keepdims=True)
    first = jnp.min(jnp.where(v == mx, row, v.shape[0]), axis=0, keepdims=True)
    return mx, row == first


def _pack_bf16_pairs(a):
    c = a.shape[1] // 2
    bits = lax.bitcast_convert_type(a.astype(BF16).astype(F32), jnp.uint32)
    return (bits[:, :c] >> 16) | (bits[:, c:] & jnp.uint32(0xFFFF0000))


def _unpack_pairs_f32(u):
    lo = lax.bitcast_convert_type(u << 16, F32)
    hi = lax.bitcast_convert_type(u & jnp.uint32(0xFFFF0000), F32)
    return lo, hi


def _unpack_bf16_pairs(u):
    return jnp.concatenate(_unpack_pairs_f32(u), axis=1).astype(BF16)


def _swiglu(x, wg, wu, wd):
    hg = _dot(x, wg)
    return _dot(((hg * _sigmoid(hg)) * _dot(x, wu)).astype(BF16), wd)


def _outproj_kernel(x_ref, att_ref, gm_ref, wo_ref, g1_ref, b1_ref, wrh_ref, wrl_ref, rb_ref,
                    h_ref, hp_ref, gates_ref, plan_ref, cnt_ref, base_ref, *, alpha):
    aw = att_ref.shape[1]
    mix = _dot(att_ref[...], wo_ref[0:aw, :]) + _dot(gm_ref[...], wo_ref[aw:, :])
    h = _layer_norm_rows(alpha * x_ref[...] + mix, g1_ref[...], b1_ref[...])
    h_ref[...] = h
    hp_ref[...] = _pack_bf16_pairs(h)
    hh = h.astype(BF16)
    hl = (h - hh.astype(F32)).astype(BF16)
    logits = _dot_nt(wrh_ref[...], hh) + _dot_nt(wrh_ref[...], hl) + _dot_nt(wrl_ref[...], hh)
    scores = _sigmoid(logits)
    biased = scores + rb_ref[...]
    ne, tm = biased.shape
    per = ne // N_EXPERT_GROUPS
    sub = lax.broadcasted_iota(jnp.int32, (per, tm), 0)
    gscore = []
    for g in range(N_EXPERT_GROUPS):
        bg = biased[g * per:(g + 1) * per, :]
        m1, at1 = _first_max_rows(bg, sub)
        m2 = jnp.max(jnp.where(at1, -jnp.inf, bg), axis=0, keepdims=True)
        gscore.append(m1 + m2)
    allowed = []
    for g in range(N_EXPERT_GROUPS):
        rank = jnp.zeros((1, tm), jnp.int32)
        for o in range(N_EXPERT_GROUPS):
            if o < g:
                rank = rank + jnp.where(gscore[o] >= gscore[g], 1, 0)
            elif o > g:
                rank = rank + jnp.where(gscore[o] > gscore[g], 1, 0)
        allowed.append(jnp.where(rank < TOPK_GROUPS, biased[g * per:(g + 1) * per, :], NEG))
    v = jnp.concatenate(allowed, axis=0)
    row = lax.broadcasted_iota(jnp.int32, (ne, tm), 0)
    chosen = jnp.zeros((ne, tm), jnp.bool_)
    for _ in range(MOE_TOP_K):
        _, at = _first_max_rows(v, row)
        chosen = chosen | at
        v = jnp.where(at, -jnp.inf, v)
    w = jnp.where(chosen, scores, 0.0)
    w = w / jnp.sum(w, axis=0, keepdims=True) * ROUTED_SCALE
    gates_ref[...] = jnp.concatenate([w, jnp.zeros((LANES - ne, tm), F32)], axis=0).T

    @pl.when(pl.program_id(0) == 0)
    def _():
        base_ref[...] = jnp.zeros(base_ref.shape, F32)

    cf = jnp.where(chosen, 1.0, 0.0)
    cb = cf.astype(BF16)
    before = jnp.where(lax.broadcasted_iota(jnp.int32, (tm, tm), 0) < lax.broadcasted_iota(jnp.int32, (tm, tm), 1),
                       1.0, 0.0).astype(BF16)
    base = base_ref[...]
    pos = base[:, 0:1] + _dot(cb, before)
    base_ref[...] = base + jnp.sum(cf, axis=1, keepdims=True)
    cnt_ref[...] = base_ref[...]
    lower = jnp.where(lax.broadcasted_iota(jnp.int32, (ne, ne), 1) < lax.broadcasted_iota(jnp.int32, (ne, ne), 0),
                      1.0, 0.0).astype(BF16)
    slot = _dot(lower, cb)
    erow = row.astype(F32)
    prow = lax.broadcasted_iota(jnp.int32, (PLAN_ROWS, tm), 0)
    plan = jnp.zeros((PLAN_ROWS, tm), F32)
    for k in range(MOE_TOP_K):
        sel = chosen & (slot == k)
        for j, val in enumerate((erow, pos, w)):
            picked = jnp.sum(jnp.where(sel, val, 0.0), axis=0, keepdims=True)
            plan = jnp.where(prow == j * MOE_TOP_K + k, picked, plan)
    plan_ref[...] = jnp.concatenate([plan, jnp.zeros((LANES - PLAN_ROWS, tm), F32)], axis=0).T


def _split_bf16(a):
    hi = a.astype(BF16)
    return hi, (a - hi.astype(F32)).astype(BF16)


def _outproj(x, att, gm, w_o, g1, b1, w_router, router_bias, alpha, tm):
    m, dm = x.shape
    ne = w_router.shape[1]
    assert m % tm == 0 and ne <= LANES and ne % N_EXPERT_GROUPS == 0 and 3 * MOE_TOP_K <= PLAN_ROWS
    assert m < 1 << 24
    wrh, wrl = _split_bf16(w_router.T)
    row = lambda c: pl.BlockSpec((tm, c), lambda i: (i, 0))
    full = lambda a: pl.BlockSpec(a.shape, lambda i: (0,) * a.ndim)
    args = (x, att, gm, w_o, g1.reshape(1, dm), b1.reshape(1, dm), wrh, wrl, router_bias.reshape(ne, 1))
    return pl.pallas_call(
        functools.partial(_outproj_kernel, alpha=alpha),
        grid=(m // tm,),
        in_specs=[row(dm), row(att.shape[1]), row(gm.shape[1])] + [full(a) for a in args[3:]],
        out_specs=[row(dm), row(dm // 2), row(LANES), row(LANES), pl.BlockSpec((ne, LANES), lambda i: (0, 0))],
        out_shape=[jax.ShapeDtypeStruct((m, dm), F32), jax.ShapeDtypeStruct((m, dm // 2), jnp.uint32),
                   jax.ShapeDtypeStruct((m, LANES), F32), jax.ShapeDtypeStruct((m, LANES), F32),
                   jax.ShapeDtypeStruct((ne, LANES), F32)],
        scratch_shapes=[pltpu.VMEM((ne, LANES), F32)],
        compiler_params=_cparams(("arbitrary",), 40),
        name="outproj_router",
    )(*args)


def _with_rows(dest_hbm, bufs, sem, step, nsteps, per_step, body):
    def copy(st, sl):
        return pltpu.make_async_copy(dest_hbm.at[pl.ds(st * per_step, per_step)], bufs[sl], sem.at[sl])

    @pl.when(step == 0)
    def _():
        copy(0, 0).start()

    for sl in range(2):
        @pl.when(step % 2 == sl)
        def _():
            @pl.when(step + 1 < nsteps)
            def _():
                copy(step + 1, 1 - sl).start()

            copy(step, sl).wait()
            body(bufs[sl])


def _dispatch_kernel(off_ref, cnt_ref, dest_hbm, hp_ref, xs_hbm, rows0, rows1, zero_ref, sem_dest, sem_row, *, tr):
    step, nsteps = pl.program_id(0), pl.num_programs(0)
    td = hp_ref.shape[0]

    def row_copy(t, d):
        return pltpu.make_async_copy(hp_ref.at[pl.ds(t, 1)], xs_hbm.at[pl.ds(d, 1)], sem_row)

    def scatter(rows):
        def issue(t, c):
            for k in range(MOE_TOP_K):
                row_copy(t, rows[t * MOE_TOP_K + k]).start(priority=k % 2)
            return c

        lax.fori_loop(0, td, issue, 0)

    def drain(t, c):
        for k in range(MOE_TOP_K):
            row_copy(0, 0).wait()
        return c

    _with_rows(dest_hbm, (rows0, rows1), sem_dest, step, nsteps, td * MOE_TOP_K, scatter)
    lax.fori_loop(0, td, drain, 0)

    @pl.when(step == nsteps - 1)
    def _():
        zero_ref[...] = jnp.zeros(zero_ref.shape, zero_ref.dtype)

        def zero_copy(d):
            return pltpu.make_async_copy(zero_ref.at[pl.ds(0, 1)], xs_hbm.at[pl.ds(d, 1)], sem_row)

        def per_expert(e, c):
            lo = off_ref[e] + cnt_ref[e]
            hi = off_ref[e] + (cnt_ref[e] + tr - 1) // tr * tr
            lax.fori_loop(lo, hi, lambda d, c2: (zero_copy(d).start(), c2)[1], 0)
            lax.fori_loop(lo, hi, lambda d, c2: (zero_copy(0).wait(), c2)[1], 0)
            return c

        lax.fori_loop(0, cnt_ref.shape[0], per_expert, 0)


def _expert_kernel(te_ref, nu_ref, x_ref, wg_ref, wu_ref, wd_ref, y_ref):
    @pl.when(pl.program_id(0) < nu_ref[0])
    def _():
        x = _unpack_bf16_pairs(x_ref[...])
        y_ref[...] = _pack_bf16_pairs(_swiglu(x, wg_ref[...], wu_ref[...], wd_ref[...]))


def _combine_kernel(dest_hbm, h_ref, hp_ref, plan_ref, ys_hbm, wsg_ref, wsu_ref, wsd_ref, g2_ref, b2_ref,
                    o_ref, rows0, rows1, ybuf, sem_dest, sem_row, *, alpha):
    step, nsteps = pl.program_id(0), pl.num_programs(0)
    tc = h_ref.shape[0]
    half = hp_ref.shape[1]

    def row_copy(t, k, d):
        return pltpu.make_async_copy(ys_hbm.at[pl.ds(d, 1)], ybuf.at[pl.ds(t, 1), pl.ds(k * half, half)], sem_row)

    def gather(rows):
        def issue(t, c):
            for k in range(MOE_TOP_K):
                row_copy(t, k, rows[t * MOE_TOP_K + k]).start(priority=k % 2)
            return c

        lax.fori_loop(0, tc, issue, 0)

    def drain(t, c):
        for k in range(MOE_TOP_K):
            row_copy(0, k, 0).wait()
        return c

    _with_rows(dest_hbm, (rows0, rows1), sem_dest, step, nsteps, tc * MOE_TOP_K, gather)
    shared = _swiglu(_unpack_bf16_pairs(hp_ref[...]), wsg_ref[...], wsu_ref[...], wsd_ref[...])
    lax.fori_loop(0, tc, drain, 0)

    plan = plan_ref[...]
    lo_acc = jnp.zeros((tc, half), F32)
    hi_acc = jnp.zeros((tc, half), F32)
    for k in range(MOE_TOP_K):
        gate = plan[:, 2 * MOE_TOP_K + k:2 * MOE_TOP_K + k + 1]
        lo, hi = _unpack_pairs_f32(ybuf[:, k * half:(k + 1) * half])
        lo_acc = lo_acc + gate * lo
        hi_acc = hi_acc + gate * hi
    ffn = jnp.concatenate([lo_acc, hi_acc], axis=1) + shared
    o_ref[...] = _layer_norm_rows(alpha * h_ref[...] + ffn, g2_ref[...], b2_ref[...])


def _moe_sparse(h, hp, plan, cnt, wg, wu, wd, wsg, wsu, wsd, g2, b2, alpha):
    m, dm = h.shape
    ne, _, ed = wg.shape
    tr, tt = MOE_ROW_TILE, MOE_TOKEN_TILE
    assert m % tt == 0 and (tt * MOE_TOP_K) % 1024 == 0
    n_tiles = (m * MOE_TOP_K) // tr + ne
    rows = n_tiles * tr

    counts = cnt[:, 0].astype(jnp.int32)
    padded = (counts + tr - 1) // tr * tr
    ends = jnp.cumsum(padded)
    off = ends - padded
    n_used = (ends[-1:] // tr).astype(jnp.int32)
    tile_start = jnp.arange(n_tiles, dtype=jnp.int32) * tr
    tile_e = jnp.minimum(jnp.sum((ends[None, :] <= tile_start[:, None]).astype(jnp.int32), axis=1), ne - 1)
    dest = (off[plan[:, 0:MOE_TOP_K].astype(jnp.int32)]
            + plan[:, MOE_TOP_K:2 * MOE_TOP_K].astype(jnp.int32)).reshape(-1)

    row_scratch = [pltpu.SMEM((tt * MOE_TOP_K,), jnp.int32)] * 2
    sems = [pltpu.SemaphoreType.DMA((2,)), pltpu.SemaphoreType.DMA(())]
    any_spec = pl.BlockSpec(memory_space=pl.ANY)

    xs = pl.pallas_call(
        functools.partial(_dispatch_kernel, tr=tr),
        grid_spec=pltpu.PrefetchScalarGridSpec(
            num_scalar_prefetch=2, grid=(m // tt,),
            in_specs=[any_spec, pl.BlockSpec((tt, dm // 2), lambda i, *_: (i, 0))],
            out_specs=any_spec,
            scratch_shapes=row_scratch + [pltpu.VMEM((8, dm // 2), jnp.uint32)] + sems),
        out_shape=jax.ShapeDtypeStruct((rows, dm // 2), jnp.uint32),
        compiler_params=_cparams(("arbitrary",), 16),
        name="moe_dispatch",
    )(off, counts, dest, hp)

    used = lambda i, te, nu: (jnp.minimum(i, nu[0] - 1), 0)
    ys = pl.pallas_call(
        _expert_kernel,
        grid_spec=pltpu.PrefetchScalarGridSpec(
            num_scalar_prefetch=2, grid=(n_tiles,),
            in_specs=[pl.BlockSpec((tr, dm // 2), used),
                      pl.BlockSpec((None, dm, ed), lambda i, te, nu: (te[i], 0, 0)),
                      pl.BlockSpec((None, dm, ed), lambda i, te, nu: (te[i], 0, 0)),
                      pl.BlockSpec((None, ed, dm), lambda i, te, nu: (te[i], 0, 0))],
            out_specs=pl.BlockSpec((tr, dm // 2), used)),
        out_shape=jax.ShapeDtypeStruct((rows, dm // 2), jnp.uint32),
        compiler_params=_cparams(("arbitrary",), 32),
        name="moe_experts",
    )(tile_e, n_used, xs, wg, wu, wd)

    row = lambda c: pl.BlockSpec((tt, c), lambda i: (i, 0))
    full = lambda a: pl.BlockSpec(a.shape, lambda i: (0,) * a.ndim)
    g2r, b2r = g2.reshape(1, dm), b2.reshape(1, dm)
    return pl.pallas_call(
        functools.partial(_combine_kernel, alpha=alpha),
        grid=(m // tt,),
        in_specs=[any_spec, row(dm), row(dm // 2), row(LANES), any_spec,
                  full(wsg), full(wsu), full(wsd), full(g2r), full(b2r)],
        out_specs=row(dm),
        scratch_shapes=row_scratch + [pltpu.VMEM((tt, MOE_TOP_K * dm // 2), jnp.uint32)] + sems,
        out_shape=jax.ShapeDtypeStruct((m, dm), F32),
        compiler_params=_cparams(("arbitrary",), 32),
        name="moe_combine",
    )(dest, h, hp, plan, ys, wsg, wsu, wsd, g2r, b2r)


def _moe_dense_kernel(h_ref, hp_ref, gates_ref, wg_ref, wu_ref, wd_ref, wsg_ref, wsu_ref, wsd_ref, g2_ref, b2_ref,
                      o_ref, acc_ref, *, alpha):
    e = pl.program_id(0)
    x = _unpack_bf16_pairs(hp_ref[...])
    hg = _dot(x, wg_ref[...])
    hu = _dot(x, wu_ref[...])
    gates = gates_ref[...]
    lane = lax.broadcasted_iota(jnp.int32, gates.shape, 1)
    gcol = jnp.sum(jnp.where(lane == e, gates, 0.0), axis=-1, keepdims=True)
    y = _dot(((hg * _sigmoid(hg)) * hu * gcol).astype(BF16), wd_ref[...])

    @pl.when(e == 0)
    def _():
        acc_ref[...] = y

    @pl.when(e > 0)
    def _():
        acc_ref[...] += y

    @pl.when(e == pl.num_programs(0) - 1)
    def _():
        ffn = acc_ref[...] + _swiglu(x, wsg_ref[...], wsu_ref[...], wsd_ref[...])
        o_ref[...] = _layer_norm_rows(alpha * h_ref[...] + ffn, g2_ref[...], b2_ref[...])


def _moe_dense(h, hp, gates, wg, wu, wd, wsg, wsu, wsd, g2, b2, alpha):
    m, dm = h.shape
    ne, _, ed = wg.shape
    full = lambda a: pl.BlockSpec(a.shape, lambda e: (0,) * a.ndim)
    g2r, b2r = g2.reshape(1, dm), b2.reshape(1, dm)
    return pl.pallas_call(
        functools.partial(_moe_dense_kernel, alpha=alpha),
        grid=(ne,),
        in_specs=[full(h), full(hp), full(gates),
                  pl.BlockSpec((None, dm, ed), lambda e: (e, 0, 0)),
                  pl.BlockSpec((None, dm, ed), lambda e: (e, 0, 0)),
                  pl.BlockSpec((None, ed, dm), lambda e: (e, 0, 0)),
                  full(wsg), full(wsu), full(wsd), full(g2r), full(b2r)],
        out_specs=full(h),
        out_shape=jax.ShapeDtypeStruct((m, dm), F32),
        scratch_shapes=[pltpu.VMEM((m, dm), F32)],
        compiler_params=_cparams(("arbitrary",), 32),
        name="moe_dense",
    )(h, hp, gates, wg, wu, wd, wsg, wsu, wsd, g2r, b2r)


def kernel(x_prompt, x_sample, cache_k, cache_v, page_table, w_in, w_o, gm_ln_g, gm_ln_b, gm_w_s, gm_b_s,
           ln1_g, ln1_b, w_router, router_bias, w_gate, w_up, w_down, ws_gate, ws_up, ws_down, ln2_g, ln2_b):
    depth = w_in.shape[0]
    heads, page, hd = cache_k.shape[2], cache_k.shape[3], cache_k.shape[4]
    n, t, dm = x_prompt.shape
    ns, ts, _ = x_sample.shape
    assert ts == 1 and MOBA_BLOCK % page == 0 and MOBA_BLOCK == 1 << BLOCK_SHIFT
    alpha = float((2 * depth) ** 0.25)
    xp, xs = x_prompt.reshape(n * t, dm), x_sample.reshape(ns, dm)
    cache_kt, cache_vt = jnp.swapaxes(cache_k, 3, 4), jnp.swapaxes(cache_v, 3, 4)
    outs = [[] for _ in range(5)]
    for l in range(depth):
        w_in_b = w_in[l].astype(BF16)
        w_o_b = w_o[l].astype(BF16)
        experts = tuple(a[l].astype(BF16) for a in (w_gate, w_up, w_down, ws_gate, ws_up, ws_down))
        gm_w = (gm_ln_g[l], gm_ln_b[l], gm_w_s[l], gm_b_s[l])
        route = (w_o_b, ln1_g[l], ln1_b[l], w_router[l], router_bias[l], alpha)
        norm2 = (ln2_g[l], ln2_b[l], alpha)

        qp, kp, vp, k_pages, v_pages, gm = _proj_prompt(xp.reshape(n, t, dm), w_in_b, *gm_w, heads, hd, page)
        att = _moba_prompt(qp, kp, vp, hd)
        h, hp, _, plan, cnt = _outproj(xp, att.reshape(n * t, heads * hd), gm.reshape(n * t, -1), *route, 512)
        xp = _moe_sparse(h, hp, plan, cnt, *experts, *norm2)
        outs[0].append(k_pages)
        outs[1].append(v_pages)

        q_s, k_s, v_s, gm_s, vn_s = _proj_sample(xs, w_in_b, *gm_w, heads, hd)
        col4 = (ns, heads, hd, 1)
        idx = _sample_gate(q_s.reshape(col4), cache_kt, l, page_table)
        att_s = _sample_attention(q_s.reshape(col4), k_s.reshape(col4), v_s.reshape(col4),
                                  cache_kt, cache_vt, l, page_table, idx)
        h, hp, gates, _, _ = _outproj(xs, att_s.reshape(ns, heads * hd).astype(BF16), gm_s, *route, ns)
        xs = _moe_dense(h, hp, gates, *experts, *norm2)
        shp4 = (ns, heads, 1, hd)
        outs[2].append(k_s.reshape(shp4))
        outs[3].append(v_s.reshape(shp4))
        outs[4].append(vn_s.reshape(ns, 1, -1))
    return (xp.reshape(n, t, dm), xs.reshape(ns, 1, dm)) + tuple(jnp.stack(o) for o in outs)
```

```python
import functools

import jax
import jax.numpy as jnp
from jax import lax
from jax.experimental import pallas as pl
from jax.experimental.pallas import tpu as pltpu

F32 = jnp.float32
BF16 = jnp.bfloat16

MOBA_BLOCK = 256
BLOCK_SHIFT = 8
MOBA_TOP_K = 3
MOBA_QBLOCKS = 4
MOE_TOP_K = 8
N_EXPERT_GROUPS = 8
TOPK_GROUPS = 4
ROUTED_SCALE = 2.5
LN_EPS = 1e-5
NEG = -1e30
MASK_BIG = float(2.0 ** 100)
LANES = 128
V7X_VMEM_BYTES = 64 * 1024 * 1024


def _cparams(semantics, vmem_mb):
    assert vmem_mb * 1024 * 1024 < V7X_VMEM_BYTES
    return pltpu.CompilerParams(dimension_semantics=semantics, vmem_limit_bytes=vmem_mb * 1024 * 1024)


def _dot(a, b):
    return jnp.dot(a, b, preferred_element_type=F32)


def _dot_nt(a, b):
    return lax.dot_general(a, b, (((1,), (1,)), ((), ())), preferred_element_type=F32)


def _layer_norm_rows(y, g, b):
    mu = jnp.mean(y, axis=-1, keepdims=True)
    var = jnp.mean(jnp.square(y - mu), axis=-1, keepdims=True)
    return (y - mu) * lax.rsqrt(var + LN_EPS) * g + b


def _sigmoid(x):
    return 1.0 / (1.0 + jnp.exp(-x))


def _gmlp_norm(zgv, lng, lnb, groups, gdim):
    vg = jax.nn.gelu(zgv)
    out = []
    for g in range(groups):
        sl = slice(g * gdim, (g + 1) * gdim)
        out.append(_layer_norm_rows(vg[:, sl], lng[:, sl], lnb[:, sl]))
    return out


def _proj_prompt_kernel(x_ref, w_ref, lng_ref, lnb_ref, ws_ref, bias_ref,
                        q_ref, k_ref, v_ref, ko_ref, vo_ref, gm_ref, *, heads, hd, groups, chunk):
    tm = x_ref.shape[1]
    aw = heads * hd
    gdim = gm_ref.shape[2] // groups
    x = x_ref[0].astype(BF16)
    zq = _dot(x, w_ref[:, 0:aw])
    zk = _dot(x, w_ref[:, aw:2 * aw])
    zv = _dot(x, w_ref[:, 2 * aw:3 * aw])

    lane = lax.broadcasted_iota(jnp.int32, (tm, LANES), 1)
    row = lax.broadcasted_iota(jnp.int32, (tm, LANES), 0)
    blk = (pl.program_id(1) * tm + row) >> BLOCK_SHIFT
    lo = lane < hd
    for p in range(heads // 2):
        sl = slice(p * LANES, (p + 1) * LANES)
        cq = zq[:, sl] * (hd ** -0.5)
        ck = zk[:, sl]
        cv = zv[:, sl]
        q_ref[0, 2 * p] = jnp.where(lo, cq, 0.0).astype(BF16)
        q_ref[0, 2 * p + 1] = jnp.where(lo, 0.0, cq).astype(BF16)
        k_ref[0, 2 * p] = jnp.where(lo, ck, jnp.where(lane - hd == blk, -MASK_BIG, 0.0)).astype(BF16)
        k_ref[0, 2 * p + 1] = jnp.where(lo, jnp.where(lane == blk, -MASK_BIG, 0.0), ck).astype(BF16)
        v_ref[0, 2 * p] = jnp.where(lo, cv, jnp.where(lane == hd, 1.0, 0.0)).astype(BF16)
        v_ref[0, 2 * p + 1] = jnp.where(lo, jnp.where(lane == 0, 1.0, 0.0), cv).astype(BF16)

    page = ko_ref.shape[3]
    for c in range(tm // page):
        for h in range(heads):
            ko_ref[0, c, h] = zk[c * page:(c + 1) * page, h * hd:(h + 1) * hd]
            vo_ref[0, c, h] = zv[c * page:(c + 1) * page, h * hd:(h + 1) * hd]

    gw = groups * gdim
    u = jax.nn.gelu(_dot(x, w_ref[:, 3 * aw:3 * aw + gw]))
    vn = _gmlp_norm(_dot(x, w_ref[:, 3 * aw + gw:3 * aw + 2 * gw]), lng_ref[...], lnb_ref[...], groups, gdim)
    ci = lax.broadcasted_iota(jnp.int32, (chunk, chunk), 0)
    cj = lax.broadcasted_iota(jnp.int32, (chunk, chunk), 1)
    for g in range(groups):
        wm = jnp.where(ci >= cj, ws_ref[g], 0.0).astype(BF16)
        gs = slice(g * gdim, (g + 1) * gdim)
        for c in range(tm // chunk):
            rs = slice(c * chunk, (c + 1) * chunk)
            s = _dot(wm, vn[g][rs].astype(BF16)) + bias_ref[:, gs]
            gm_ref[0, rs, gs] = (u[rs, gs] * s).astype(BF16)


def _proj_sample_kernel(x_ref, w_ref, lng_ref, lnb_ref, w00_ref, b0_ref,
                        q_ref, k_ref, v_ref, gm_ref, vn_ref, *, heads, hd, groups):
    aw = heads * hd
    gw = vn_ref.shape[1]
    gdim = gw // groups
    x = x_ref[...].astype(BF16)
    q_ref[...] = _dot(x, w_ref[:, 0:aw]) * (hd ** -0.5)
    k_ref[...] = _dot(x, w_ref[:, aw:2 * aw])
    v_ref[...] = _dot(x, w_ref[:, 2 * aw:3 * aw])
    u = jax.nn.gelu(_dot(x, w_ref[:, 3 * aw:3 * aw + gw]))
    vn = _gmlp_norm(_dot(x, w_ref[:, 3 * aw + gw:3 * aw + 2 * gw]), lng_ref[...], lnb_ref[...], groups, gdim)
    for g in range(groups):
        gs = slice(g * gdim, (g + 1) * gdim)
        vn_ref[:, gs] = vn[g]
        gm_ref[:, gs] = (u[:, gs] * (w00_ref[:, gs] * vn[g] + b0_ref[:, gs])).astype(BF16)


def _proj_prompt(x, w_in, lng, lnb, w_s, b_s, heads, hd, page):
    n, t, dm = x.shape
    groups, chunk = w_s.shape[0], w_s.shape[1]
    gw = lng.shape[-1]
    gdim = gw // groups
    tm = 512
    assert t % tm == 0 and tm % MOBA_BLOCK == 0 and tm % page == 0 and tm % chunk == 0
    assert heads % 2 == 0 and 2 * hd == LANES and t // MOBA_BLOCK <= hd
    bias = jnp.repeat(b_s.T, gdim, axis=1)
    att_shape = jax.ShapeDtypeStruct((n, heads, t, LANES), BF16)
    page_shape = jax.ShapeDtypeStruct((n, t // page, heads, page, hd), F32)
    att_spec = pl.BlockSpec((1, heads, tm, LANES), lambda b, i: (b, 0, i, 0))
    page_spec = pl.BlockSpec((1, tm // page, heads, page, hd), lambda b, i: (b, i, 0, 0, 0))
    full = lambda a: pl.BlockSpec(a.shape, lambda b, i: (0,) * a.ndim)
    lng2, lnb2 = lng.reshape(1, gw), lnb.reshape(1, gw)
    return pl.pallas_call(
        functools.partial(_proj_prompt_kernel, heads=heads, hd=hd, groups=groups, chunk=chunk),
        grid=(n, t // tm),
        in_specs=[pl.BlockSpec((1, tm, dm), lambda b, i: (b, i, 0)), full(w_in), full(lng2), full(lnb2),
                  full(w_s), full(bias)],
        out_specs=[att_spec, att_spec, att_spec, page_spec, page_spec,
                   pl.BlockSpec((1, tm, gw), lambda b, i: (b, i, 0))],
        out_shape=[att_shape, att_shape, att_shape, page_shape, page_shape,
                   jax.ShapeDtypeStruct((n, t, gw), BF16)],
        compiler_params=_cparams(("parallel", "parallel"), 48),
        name="proj_prompt",
    )(x, w_in, lng2, lnb2, w_s, bias)


def _proj_sample(x, w_in, lng, lnb, w_s, b_s, heads, hd):
    m, dm = x.shape
    groups = w_s.shape[0]
    gw = lng.shape[-1]
    gdim = gw // groups
    aw = heads * hd
    w00 = jnp.repeat(w_s[:, 0, 0], gdim).reshape(1, gw)
    b0 = jnp.repeat(b_s[:, 0], gdim).reshape(1, gw)
    f = lambda c, dt: jax.ShapeDtypeStruct((m, c), dt)
    return pl.pallas_call(
        functools.partial(_proj_sample_kernel, heads=heads, hd=hd, groups=groups),
        out_shape=[f(aw, F32), f(aw, F32), f(aw, F32), f(gw, BF16), f(gw, F32)],
        compiler_params=_cparams(None, 32),
        name="proj_sample",
    )(x, w_in, lng.reshape(1, gw), lnb.reshape(1, gw), w00, b0)


def _moba_kernel(q_ref, k_ref, v_ref, o_ref, kmh_ref, kml_ref, q2_ref, m_ref, acc_ref, *, hd):
    t = q_ref.shape[2]
    bq = MOBA_BLOCK
    nb = t // bq
    rows = MOBA_QBLOCKS * bq
    nbp = -(-nb // 8) * 8
    aux0 = (hd, 0)

    key_blk = lax.broadcasted_iota(jnp.int32, (LANES, t), 1) >> BLOCK_SHIFT
    out_row = lax.broadcasted_iota(jnp.int32, (LANES, t), 0)
    lane_sq = lax.broadcasted_iota(jnp.int32, (LANES, LANES), 1)
    for hh in range(2):
        avg = jnp.where(out_row - aux0[hh] == key_blk, 1.0 / bq, 0.0).astype(BF16)
        km_rows = _dot(avg, k_ref[0, hh])
        km_rows = jnp.where(lane_sq < hd if hh == 0 else lane_sq >= hd, km_rows, 0.0)
        hi = km_rows.astype(BF16)
        kmh_ref[hh] = hi
        kml_ref[hh] = (km_rows - hi.astype(F32)).astype(BF16)

    lane = lax.broadcasted_iota(jnp.int32, (rows, LANES), 1)
    g_blk = lax.broadcasted_iota(jnp.int32, (nbp, rows), 0)
    g_qblk = lax.broadcasted_iota(jnp.int32, (nbp, rows), 1) >> BLOCK_SHIFT

    def update(hh, j, r_lo, causal):
        c0 = pl.multiple_of(j * bq, bq)
        s = _dot_nt(q2_ref[hh, r_lo:, :], k_ref[0, hh, pl.ds(c0, bq), :])
        if causal:
            rr = lax.broadcasted_iota(jnp.int32, s.shape, 0)
            cc = lax.broadcasted_iota(jnp.int32, s.shape, 1)
            s = jnp.where(cc <= rr, s, NEG)
        m_prev = m_ref[hh, r_lo:, :]
        m_new = jnp.maximum(m_prev, jnp.max(s, axis=-1, keepdims=True))
        alpha = jnp.exp(m_prev - m_new)
        p = jnp.exp(s - jnp.concatenate([m_new] * (bq // LANES), axis=1))
        m_ref[hh, r_lo:, :] = m_new
        acc_ref[hh, r_lo:, :] = alpha * acc_ref[hh, r_lo:, :] + _dot(p.astype(BF16), v_ref[0, hh, pl.ds(c0, bq), :])

    def q_rows(sb, carry):
        r0 = pl.multiple_of(sb * rows, rows)
        b0 = sb * MOBA_QBLOCKS
        for hh in range(2):
            q = q_ref[0, hh, pl.ds(r0, rows), :]
            gate = _dot_nt(kmh_ref[hh], q) + _dot_nt(kml_ref[hh], q)
            past = g_blk < b0 + g_qblk
            g = jnp.where(past, gate[aux0[hh]:aux0[hh] + nbp, :], NEG)
            notsel = jnp.where(past, 1.0, 0.0)
            for _ in range(MOBA_TOP_K):
                mx, at = _first_max_rows(g, g_blk)
                pick = at & (mx > 0.5 * NEG)
                notsel = jnp.where(pick, 0.0, notsel)
                g = jnp.where(pick, NEG, g)
            pieces = [jnp.zeros((aux0[hh], rows), F32)] if aux0[hh] else []
            pieces += [notsel, jnp.zeros((LANES - aux0[hh] - nbp, rows), F32)]
            q2_ref[hh] = q + jnp.concatenate(pieces, axis=0).T.astype(BF16)
            m_ref[hh] = jnp.full((rows, LANES), NEG, F32)
            acc_ref[hh] = jnp.zeros((rows, LANES), F32)

        def past_blocks(g, c):
            for d in range(MOBA_QBLOCKS):
                for hh in range(2):
                    update(hh, g * MOBA_QBLOCKS + d, 0, False)
            return c

        lax.fori_loop(0, sb, past_blocks, 0)
        for d in range(MOBA_QBLOCKS):
            for hh in range(2):
                update(hh, b0 + d, d * bq, True)
        a0 = acc_ref[0]
        a1 = acc_ref[1]
        o0 = a0 * (1.0 / a0[:, hd:hd + 1])
        o1 = a1 * (1.0 / a1[:, 0:1])
        o_ref[0, pl.ds(r0, rows), :] = jnp.where(lane < hd, o0, o1).astype(o_ref.dtype)
        return carry

    lax.fori_loop(0, nb // MOBA_QBLOCKS, q_rows, 0)


def _moba_prompt(qp, kp, vp, hd):
    n, heads, t, _ = qp.shape
    rows = MOBA_QBLOCKS * MOBA_BLOCK
    assert t % rows == 0 and MOBA_BLOCK % LANES == 0
    spec = pl.BlockSpec((1, 2, t, LANES), lambda b, p: (b, p, 0, 0))
    return pl.pallas_call(
        functools.partial(_moba_kernel, hd=hd),
        grid=(n, heads // 2),
        in_specs=[spec, spec, spec],
        out_specs=pl.BlockSpec((1, t, LANES), lambda b, p: (b, 0, p)),
        out_shape=jax.ShapeDtypeStruct((n, t, heads * hd), BF16),
        scratch_shapes=[pltpu.VMEM((2, LANES, LANES), BF16), pltpu.VMEM((2, LANES, LANES), BF16),
                        pltpu.VMEM((2, rows, LANES), BF16),
                        pltpu.VMEM((2, rows, LANES), F32), pltpu.VMEM((2, rows, LANES), F32)],
        compiler_params=_cparams(("parallel", "parallel"), 48),
        name="moba_prompt",
    )(qp, kp, vp)


SGATE_PAGES = 16


def _sgate_kernel(pt_ref, q_ref, *refs, ppb):
    pages = refs[:SGATE_PAGES]
    idx_ref = refs[SGATE_PAGES]
    gate_ref = refs[SGATE_PAGES + 1]
    s = pl.program_id(1)
    bps = SGATE_PAGES // ppb
    qb = jnp.broadcast_to(q_ref[0], pages[0].shape)
    rows = []
    for b in range(bps):
        acc = pages[b * ppb][...]
        for c in range(1, ppb):
            acc = acc + pages[b * ppb + c][...]
        per_row = jnp.sum(acc * qb, axis=1)
        rows.append(jnp.sum(per_row, axis=1, keepdims=True) / MOBA_BLOCK)
    gate_ref[s] = jnp.concatenate(rows, axis=1)

    @pl.when(s == pl.num_programs(1) - 1)
    def _():
        nsteps = gate_ref.shape[0]
        g = jnp.concatenate([gate_ref[i] for i in range(nsteps)], axis=1)
        col = lax.broadcasted_iota(jnp.int32, g.shape, 1)
        picks = []
        for _ in range(MOBA_TOP_K):
            mx = jnp.max(g, axis=-1, keepdims=True)
            first = jnp.min(jnp.where(g == mx, col, g.shape[1]), axis=-1, keepdims=True)
            picks.append(first)
            g = jnp.where(col == first, NEG, g)
        picks.append(jnp.zeros((g.shape[0], LANES - MOBA_TOP_K), jnp.int32))
        idx_ref[0] = jnp.concatenate(picks, axis=1)


def _sample_gate(q4, cache_kt, layer, page_table):
    n, heads, hd, _ = q4.shape
    page = cache_kt.shape[4]
    n_pages = page_table.shape[1]
    ppb = MOBA_BLOCK // page
    assert n_pages % SGATE_PAGES == 0 and SGATE_PAGES % ppb == 0 and n_pages // ppb >= MOBA_TOP_K
    steps = n_pages // SGATE_PAGES

    def page_spec(c):
        return pl.BlockSpec((None, None, heads, hd, page),
                            lambda b, s, pt: (layer, pt[b * n_pages + s * SGATE_PAGES + c], 0, 0, 0))

    grid_spec = pltpu.PrefetchScalarGridSpec(
        num_scalar_prefetch=1,
        grid=(n, steps),
        in_specs=[pl.BlockSpec((1, heads, hd, 1), lambda b, s, pt: (b, 0, 0, 0))]
                 + [page_spec(c) for c in range(SGATE_PAGES)],
        out_specs=pl.BlockSpec((1, heads, LANES), lambda b, s, pt: (b, 0, 0)),
        scratch_shapes=[pltpu.VMEM((steps, heads, SGATE_PAGES // ppb), F32)],
    )
    return pl.pallas_call(
        functools.partial(_sgate_kernel, ppb=ppb),
        grid_spec=grid_spec,
        out_shape=jax.ShapeDtypeStruct((n, heads, LANES), jnp.int32),
        compiler_params=_cparams(("parallel", "arbitrary"), 40),
        name="sample_gate",
    )(page_table.reshape(-1), q4, *([cache_kt] * SGATE_PAGES))


SATTN_HEADS = 4


def _sattn_kernel(pt_ref, ix_ref, q_ref, kn_ref, vn_ref, *refs, npg):
    o_ref = refs[2 * npg * SATTN_HEADS]
    for hh in range(SATTN_HEADS):
        kp = refs[hh * npg:(hh + 1) * npg]
        vp = refs[(SATTN_HEADS + hh) * npg:(SATTN_HEADS + hh + 1) * npg]
        q = q_ref[0, hh]
        s_new = jnp.sum(q * kn_ref[0, hh], axis=0, keepdims=True)
        logits = [jnp.sum(kp[j][...] * q, axis=0, keepdims=True) for j in range(npg)]
        mx = s_new
        for lg in logits:
            mx = jnp.maximum(mx, jnp.max(lg, axis=1, keepdims=True))
        p_new = jnp.exp(s_new - mx)
        den = p_new
        num = p_new * vn_ref[0, hh]
        for j in range(npg):
            p = jnp.exp(logits[j] - mx)
            den = den + jnp.sum(p, axis=1, keepdims=True)
            num = num + jnp.sum(vp[j][...] * p, axis=1, keepdims=True)
        o_ref[0, hh] = num / den


def _sample_attention(q4, k4, v4, cache_kt, cache_vt, layer, page_table, idx):
    n, heads, hd, _ = q4.shape
    page = cache_kt.shape[4]
    n_pages = page_table.shape[1]
    ppb = MOBA_BLOCK // page
    npg = MOBA_TOP_K * ppb

    hs = SATTN_HEADS
    assert heads % hs == 0

    def page_spec(hh, j):
        r, c = divmod(j, ppb)

        def index(b, g, pt, ix):
            h = g * hs + hh
            return (layer, pt[b * n_pages + ix[(b * heads + h) * MOBA_TOP_K + r] * ppb + c], h, 0, 0)

        return pl.BlockSpec((None, None, None, hd, page), index)

    pages = [page_spec(hh, j) for hh in range(hs) for j in range(npg)]
    tok = pl.BlockSpec((1, hs, hd, 1), lambda b, g, pt, ix: (b, g, 0, 0))
    grid_spec = pltpu.PrefetchScalarGridSpec(
        num_scalar_prefetch=2,
        grid=(n, heads // hs),
        in_specs=[tok, tok, tok] + pages * 2,
        out_specs=tok,
    )
    return pl.pallas_call(
        functools.partial(_sattn_kernel, npg=npg),
        grid_spec=grid_spec,
        out_shape=jax.ShapeDtypeStruct((n, heads, hd, 1), F32),
        compiler_params=_cparams(("parallel", "parallel"), 16),
        name="sample_attention",
    )(page_table.reshape(-1), idx[:, :, :MOBA_TOP_K].reshape(-1), q4, k4, v4,
      *([cache_kt] * (npg * hs)), *([cache_vt] * (npg * hs)))


MOE_ROW_TILE = 512
MOE_TOKEN_TILE = 512
PLAN_ROWS = 32


def _first_max_rows(v, row):
    mx = jnp.max(v, axis=0, keepdims=True)
    first = jnp.min(jnp.where(v == mx, row, v.shape[0]), axis=0, keepdims=True)
    return mx, row == first


def _pack_bf16_pairs(a):
    c = a.shape[1] // 2
    bits = lax.bitcast_convert_type(a.astype(BF16).astype(F32), jnp.uint32)
    return (bits[:, :c] >> 16) | (bits[:, c:] & jnp.uint32(0xFFFF0000))


def _unpack_pairs_f32(u):
    lo = lax.bitcast_convert_type(u << 16, F32)
    hi = lax.bitcast_convert_type(u & jnp.uint32(0xFFFF0000), F32)
    return lo, hi


def _unpack_bf16_pairs(u):
    return jnp.concatenate(_unpack_pairs_f32(u), axis=1).astype(BF16)


def _store_row_chunks(ref, packed):
    m, nc = packed.shape[0], packed.shape[1] // LANES
    for c in range(nc):
        ref[pl.ds(c, m, stride=nc), :] = packed[:, c * LANES:(c + 1) * LANES]


def _load_row_chunks(ref, nc):
    m = ref.shape[0] // nc
    return jnp.concatenate([ref[pl.ds(c, m, stride=nc), :] for c in range(nc)], axis=1)


def _swiglu(x, wg, wu, wd):
    hg = _dot(x, wg)
    return _dot(((hg * _sigmoid(hg)) * _dot(x, wu)).astype(BF16), wd)


def _outproj_kernel(x_ref, att_ref, gm_ref, wo_ref, g1_ref, b1_ref, wrh_ref, wrl_ref, rb_ref,
                    h_ref, hp_ref, gates_ref, plan_ref, cnt_ref, base_ref, *, alpha):
    aw = att_ref.shape[1]
    mix = _dot(att_ref[...], wo_ref[0:aw, :]) + _dot(gm_ref[...], wo_ref[aw:, :])
    h = _layer_norm_rows(alpha * x_ref[...] + mix, g1_ref[...], b1_ref[...])
    h_ref[...] = h
    _store_row_chunks(hp_ref, _pack_bf16_pairs(h))
    hh = h.astype(BF16)
    hl = (h - hh.astype(F32)).astype(BF16)
    logits = _dot_nt(wrh_ref[...], hh) + _dot_nt(wrh_ref[...], hl) + _dot_nt(wrl_ref[...], hh)
    scores = _sigmoid(logits)
    biased = scores + rb_ref[...]
    ne, tm = biased.shape
    per = ne // N_EXPERT_GROUPS
    sub = lax.broadcasted_iota(jnp.int32, (per, tm), 0)
    gscore = []
    for g in range(N_EXPERT_GROUPS):
        bg = biased[g * per:(g + 1) * per, :]
        m1, at1 = _first_max_rows(bg, sub)
        m2 = jnp.max(jnp.where(at1, -jnp.inf, bg), axis=0, keepdims=True)
        gscore.append(m1 + m2)
    allowed = []
    for g in range(N_EXPERT_GROUPS):
        rank = jnp.zeros((1, tm), jnp.int32)
        for o in range(N_EXPERT_GROUPS):
            if o < g:
                rank = rank + jnp.where(gscore[o] >= gscore[g], 1, 0)
            elif o > g:
                rank = rank + jnp.where(gscore[o] > gscore[g], 1, 0)
        allowed.append(jnp.where(rank < TOPK_GROUPS, biased[g * per:(g + 1) * per, :], NEG))
    v = jnp.concatenate(allowed, axis=0)
    row = lax.broadcasted_iota(jnp.int32, (ne, tm), 0)
    chosen = jnp.zeros((ne, tm), jnp.bool_)
    for _ in range(MOE_TOP_K):
        _, at = _first_max_rows(v, row)
        chosen = chosen | at
        v = jnp.where(at, -jnp.inf, v)
    w = jnp.where(chosen, scores, 0.0)
    w = w / jnp.sum(w, axis=0, keepdims=True) * ROUTED_SCALE
    gates_ref[...] = jnp.concatenate([w, jnp.zeros((LANES - ne, tm), F32)], axis=0).T

    @pl.when(pl.program_id(0) == 0)
    def _():
        base_ref[...] = jnp.zeros(base_ref.shape, F32)

    cf = jnp.where(chosen, 1.0, 0.0)
    cb = cf.astype(BF16)
    before = jnp.where(lax.broadcasted_iota(jnp.int32, (tm, tm), 0) < lax.broadcasted_iota(jnp.int32, (tm, tm), 1),
                       1.0, 0.0).astype(BF16)
    base = base_ref[...]
    pos = base[:, 0:1] + _dot(cb, before)
    base_ref[...] = base + jnp.sum(cf, axis=1, keepdims=True)
    cnt_ref[...] = base_ref[...]
    lower = jnp.where(lax.broadcasted_iota(jnp.int32, (ne, ne), 1) < lax.broadcasted_iota(jnp.int32, (ne, ne), 0),
                      1.0, 0.0).astype(BF16)
    slot = _dot(lower, cb)
    erow = row.astype(F32)
    prow = lax.broadcasted_iota(jnp.int32, (PLAN_ROWS, tm), 0)
    plan = jnp.zeros((PLAN_ROWS, tm), F32)
    for k in range(MOE_TOP_K):
        sel = chosen & (slot == k)
        for j, val in enumerate((erow, pos, w)):
            picked = jnp.sum(jnp.where(sel, val, 0.0), axis=0, keepdims=True)
            plan = jnp.where(prow == j * MOE_TOP_K + k, picked, plan)
    plan_ref[...] = jnp.concatenate([plan, jnp.zeros((LANES - PLAN_ROWS, tm), F32)], axis=0).T


def _split_bf16(a):
    hi = a.astype(BF16)
    return hi, (a - hi.astype(F32)).astype(BF16)


def _outproj(x, att, gm, w_o, g1, b1, w_router, router_bias, alpha, tm):
    m, dm = x.shape
    ne = w_router.shape[1]
    assert m % tm == 0 and ne <= LANES and ne % N_EXPERT_GROUPS == 0 and 3 * MOE_TOP_K <= PLAN_ROWS
    assert m < 1 << 24 and dm % (2 * LANES) == 0
    nc = dm // 2 // LANES
    wrh, wrl = _split_bf16(w_router.T)
    row = lambda c: pl.BlockSpec((tm, c), lambda i: (i, 0))
    full = lambda a: pl.BlockSpec(a.shape, lambda i: (0,) * a.ndim)
    args = (x, att, gm, w_o, g1.reshape(1, dm), b1.reshape(1, dm), wrh, wrl, router_bias.reshape(ne, 1))
    return pl.pallas_call(
        functools.partial(_outproj_kernel, alpha=alpha),
        grid=(m // tm,),
        in_specs=[row(dm), row(att.shape[1]), row(gm.shape[1])] + [full(a) for a in args[3:]],
        out_specs=[row(dm), pl.BlockSpec((tm * nc, LANES), lambda i: (i, 0)), row(LANES), row(LANES),
                   pl.BlockSpec((ne, LANES), lambda i: (0, 0))],
        out_shape=[jax.ShapeDtypeStruct((m, dm), F32), jax.ShapeDtypeStruct((m * nc, LANES), jnp.uint32),
                   jax.ShapeDtypeStruct((m, LANES), F32), jax.ShapeDtypeStruct((m, LANES), F32),
                   jax.ShapeDtypeStruct((ne, LANES), F32)],
        scratch_shapes=[pltpu.VMEM((ne, LANES), F32)],
        compiler_params=_cparams(("arbitrary",), 40),
        name="outproj_router",
    )(*args)


def _with_rows(dest_hbm, bufs, sem, step, nsteps, per_step, body):
    def copy(st, sl):
        return pltpu.make_async_copy(dest_hbm.at[pl.ds(st * per_step, per_step)], bufs[sl], sem.at[sl])

    @pl.when(step == 0)
    def _():
        copy(0, 0).start()

    for sl in range(2):
        @pl.when(step % 2 == sl)
        def _():
            @pl.when(step + 1 < nsteps)
            def _():
                copy(step + 1, 1 - sl).start()

            copy(step, sl).wait()
            body(bufs[sl])


def _dispatch_kernel(off_ref, cnt_ref, dest_hbm, hp_ref, xs_hbm, rows0, rows1, zero_ref, sem_dest, sem_row, *, tr, nc):
    step, nsteps = pl.program_id(0), pl.num_programs(0)
    td = hp_ref.shape[0] // nc

    def chunk_rows(r):
        return pl.ds(pl.multiple_of(r * nc, nc), nc)

    def row_copy(t, d):
        return pltpu.make_async_copy(hp_ref.at[chunk_rows(t)], xs_hbm.at[chunk_rows(d)], sem_row)

    def scatter(rows):
        def issue(t, c):
            for k in range(MOE_TOP_K):
                row_copy(t, rows[t * MOE_TOP_K + k]).start(priority=k % 2)
            return c

        lax.fori_loop(0, td, issue, 0)

    def drain(t, c):
        for k in range(MOE_TOP_K):
            row_copy(0, 0).wait()
        return c

    _with_rows(dest_hbm, (rows0, rows1), sem_dest, step, nsteps, td * MOE_TOP_K, scatter)
    lax.fori_loop(0, td, drain, 0)

    @pl.when(step == nsteps - 1)
    def _():
        zero_ref[...] = jnp.zeros(zero_ref.shape, zero_ref.dtype)

        def zero_copy(d):
            return pltpu.make_async_copy(zero_ref.at[pl.ds(0, nc)], xs_hbm.at[chunk_rows(d)], sem_row)

        def per_expert(e, c):
            lo = off_ref[e] + cnt_ref[e]
            hi = off_ref[e] + (cnt_ref[e] + tr - 1) // tr * tr
            lax.fori_loop(lo, hi, lambda d, c2: (zero_copy(d).start(), c2)[1], 0)
            lax.fori_loop(lo, hi, lambda d, c2: (zero_copy(0).wait(), c2)[1], 0)
            return c

        lax.fori_loop(0, cnt_ref.shape[0], per_expert, 0)


def _expert_kernel(te_ref, nu_ref, x_ref, wg_ref, wu_ref, wd_ref, y_ref, *, nc):
    @pl.when(pl.program_id(0) < nu_ref[0])
    def _():
        x = _unpack_bf16_pairs(_load_row_chunks(x_ref, nc))
        _store_row_chunks(y_ref, _pack_bf16_pairs(_swiglu(x, wg_ref[...], wu_ref[...], wd_ref[...])))


def _combine_kernel(dest_hbm, h_ref, hp_ref, plan_ref, ys_hbm, wsg_ref, wsu_ref, wsd_ref, g2_ref, b2_ref,
                    o_ref, rows0, rows1, ybuf, sem_dest, sem_row, *, alpha, nc):
    step, nsteps = pl.program_id(0), pl.num_programs(0)
    tc = h_ref.shape[0]

    def chunk_rows(r):
        return pl.ds(pl.multiple_of(r * nc, nc), nc)

    def row_copy(t, k, d):
        return pltpu.make_async_copy(ys_hbm.at[chunk_rows(d)], ybuf.at[k, chunk_rows(t)], sem_row)

    def gather(rows):
        def issue(t, c):
            for k in range(MOE_TOP_K):
                row_copy(t, k, rows[t * MOE_TOP_K + k]).start(priority=k % 2)
            return c

        lax.fori_loop(0, tc, issue, 0)

    def drain(t, c):
        for k in range(MOE_TOP_K):
            row_copy(0, k, 0).wait()
        return c

    _with_rows(dest_hbm, (rows0, rows1), sem_dest, step, nsteps, tc * MOE_TOP_K, gather)
    x = _unpack_bf16_pairs(_load_row_chunks(hp_ref, nc))
    shared = _swiglu(x, wsg_ref[...], wsu_ref[...], wsd_ref[...])
    lax.fori_loop(0, tc, drain, 0)

    plan = plan_ref[...]
    gates = [plan[:, 2 * MOE_TOP_K + k:2 * MOE_TOP_K + k + 1] for k in range(MOE_TOP_K)]
    lo_acc, hi_acc = [], []
    for c in range(nc):
        lo_c = jnp.zeros((tc, LANES), F32)
        hi_c = jnp.zeros((tc, LANES), F32)
        for k in range(MOE_TOP_K):
            lo, hi = _unpack_pairs_f32(ybuf[k, pl.ds(c, tc, stride=nc), :])
            lo_c = lo_c + gates[k] * lo
            hi_c = hi_c + gates[k] * hi
        lo_acc.append(lo_c)
        hi_acc.append(hi_c)
    ffn = jnp.concatenate(lo_acc + hi_acc, axis=1) + shared
    o_ref[...] = _layer_norm_rows(alpha * h_ref[...] + ffn, g2_ref[...], b2_ref[...])


def _moe_sparse(h, hp, plan, cnt, wg, wu, wd, wsg, wsu, wsd, g2, b2, alpha):
    m, dm = h.shape
    ne, _, ed = wg.shape
    tr, tt = MOE_ROW_TILE, MOE_TOKEN_TILE
    assert m % tt == 0 and (tt * MOE_TOP_K) % 1024 == 0
    n_tiles = (m * MOE_TOP_K) // tr + ne
    rows = n_tiles * tr

    counts = cnt[:, 0].astype(jnp.int32)
    padded = (counts + tr - 1) // tr * tr
    ends = jnp.cumsum(padded)
    off = ends - padded
    n_used = (ends[-1:] // tr).astype(jnp.int32)
    tile_start = jnp.arange(n_tiles, dtype=jnp.int32) * tr
    tile_e = jnp.minimum(jnp.sum((ends[None, :] <= tile_start[:, None]).astype(jnp.int32), axis=1), ne - 1)
    dest = (off[plan[:, 0:MOE_TOP_K].astype(jnp.int32)]
            + plan[:, MOE_TOP_K:2 * MOE_TOP_K].astype(jnp.int32)).reshape(-1)

    row_scratch = [pltpu.SMEM((tt * MOE_TOP_K,), jnp.int32)] * 2
    sems = [pltpu.SemaphoreType.DMA((2,)), pltpu.SemaphoreType.DMA(())]
    any_spec = pl.BlockSpec(memory_space=pl.ANY)

    nc = dm // 2 // LANES
    xs = pl.pallas_call(
        functools.partial(_dispatch_kernel, tr=tr, nc=nc),
        grid_spec=pltpu.PrefetchScalarGridSpec(
            num_scalar_prefetch=2, grid=(m // tt,),
            in_specs=[any_spec, pl.BlockSpec((tt * nc, LANES), lambda i, *_: (i, 0))],
            out_specs=any_spec,
            scratch_shapes=row_scratch + [pltpu.VMEM((8, LANES), jnp.uint32)] + sems),
        out_shape=jax.ShapeDtypeStruct((rows * nc, LANES), jnp.uint32),
        compiler_params=_cparams(("arbitrary",), 16),
        name="moe_dispatch",
    )(off, counts, dest, hp)

    used = lambda i, te, nu: (jnp.minimum(i, nu[0] - 1), 0)
    ys = pl.pallas_call(
        functools.partial(_expert_kernel, nc=nc),
        grid_spec=pltpu.PrefetchScalarGridSpec(
            num_scalar_prefetch=2, grid=(n_tiles,),
            in_specs=[pl.BlockSpec((tr * nc, LANES), used),
                      pl.BlockSpec((None, dm, ed), lambda i, te, nu: (te[i], 0, 0)),
                      pl.BlockSpec((None, dm, ed), lambda i, te, nu: (te[i], 0, 0)),
                      pl.BlockSpec((None, ed, dm), lambda i, te, nu: (te[i], 0, 0))],
            out_specs=pl.BlockSpec((tr * nc, LANES), used)),
        out_shape=jax.ShapeDtypeStruct((rows * nc, LANES), jnp.uint32),
        compiler_params=_cparams(("arbitrary",), 32),
        name="moe_experts",
    )(tile_e, n_used, xs, wg, wu, wd)

    row = lambda c: pl.BlockSpec((tt, c), lambda i: (i, 0))
    full = lambda a: pl.BlockSpec(a.shape, lambda i: (0,) * a.ndim)
    g2r, b2r = g2.reshape(1, dm), b2.reshape(1, dm)
    return pl.pallas_call(
        functools.partial(_combine_kernel, alpha=alpha, nc=nc),
        grid=(m // tt,),
        in_specs=[any_spec, row(dm), pl.BlockSpec((tt * nc, LANES), lambda i: (i, 0)), row(LANES), any_spec,
                  full(wsg), full(wsu), full(wsd), full(g2r), full(b2r)],
        out_specs=row(dm),
        scratch_shapes=row_scratch + [pltpu.VMEM((MOE_TOP_K, tt * nc, LANES), jnp.uint32)] + sems,
        out_shape=jax.ShapeDtypeStruct((m, dm), F32),
        compiler_params=_cparams(("arbitrary",), 32),
        name="moe_combine",
    )(dest, h, hp, plan, ys, wsg, wsu, wsd, g2r, b2r)


def _moe_dense_kernel(h_ref, hp_ref, gates_ref, wg_ref, wu_ref, wd_ref, wsg_ref, wsu_ref, wsd_ref, g2_ref, b2_ref,
                      o_ref, acc_ref, *, alpha):
    e = pl.program_id(0)
    x = _unpack_bf16_pairs(_load_row_chunks(hp_ref, hp_ref.shape[0] // h_ref.shape[0]))
    hg = _dot(x, wg_ref[...])
    hu = _dot(x, wu_ref[...])
    gates = gates_ref[...]
    lane = lax.broadcasted_iota(jnp.int32, gates.shape, 1)
    gcol = jnp.sum(jnp.where(lane == e, gates, 0.0), axis=-1, keepdims=True)
    y = _dot(((hg * _sigmoid(hg)) * hu * gcol).astype(BF16), wd_ref[...])

    @pl.when(e == 0)
    def _():
        acc_ref[...] = y

    @pl.when(e > 0)
    def _():
        acc_ref[...] += y

    @pl.when(e == pl.num_programs(0) - 1)
    def _():
        ffn = acc_ref[...] + _swiglu(x, wsg_ref[...], wsu_ref[...], wsd_ref[...])
        o_ref[...] = _layer_norm_rows(alpha * h_ref[...] + ffn, g2_ref[...], b2_ref[...])


def _moe_dense(h, hp, gates, wg, wu, wd, wsg, wsu, wsd, g2, b2, alpha):
    m, dm = h.shape
    ne, _, ed = wg.shape
    full = lambda a: pl.BlockSpec(a.shape, lambda e: (0,) * a.ndim)
    g2r, b2r = g2.reshape(1, dm), b2.reshape(1, dm)
    return pl.pallas_call(
        functools.partial(_moe_dense_kernel, alpha=alpha),
        grid=(ne,),
        in_specs=[full(h), full(hp), full(gates),
                  pl.BlockSpec((None, dm, ed), lambda e: (e, 0, 0)),
                  pl.BlockSpec((None, dm, ed), lambda e: (e, 0, 0)),
                  pl.BlockSpec((None, ed, dm), lambda e: (e, 0, 0)),
                  full(wsg), full(wsu), full(wsd), full(g2r), full(b2r)],
        out_specs=full(h),
        out_shape=jax.ShapeDtypeStruct((m, dm), F32),
        scratch_shapes=[pltpu.VMEM((m, dm), F32)],
        compiler_params=_cparams(("arbitrary",), 32),
        name="moe_dense",
    )(h, hp, gates, wg, wu, wd, wsg, wsu, wsd, g2r, b2r)


def kernel(x_prompt, x_sample, cache_k, cache_v, page_table, w_in, w_o, gm_ln_g, gm_ln_b, gm_w_s, gm_b_s,
           ln1_g, ln1_b, w_router, router_bias, w_gate, w_up, w_down, ws_gate, ws_up, ws_down, ln2_g, ln2_b):
    depth = w_in.shape[0]
    heads, page, hd = cache_k.shape[2], cache_k.shape[3], cache_k.shape[4]
    n, t, dm = x_prompt.shape
    ns, ts, _ = x_sample.shape
    assert ts == 1 and MOBA_BLOCK % page == 0 and MOBA_BLOCK == 1 << BLOCK_SHIFT
    alpha = float((2 * depth) ** 0.25)
    xp, xs = x_prompt.reshape(n * t, dm), x_sample.reshape(ns, dm)
    cache_kt, cache_vt = jnp.swapaxes(cache_k, 3, 4), jnp.swapaxes(cache_v, 3, 4)
    outs = [[] for _ in range(5)]
    for l in range(depth):
        w_in_b = w_in[l].astype(BF16)
        w_o_b = w_o[l].astype(BF16)
        experts = tuple(a[l].astype(BF16) for a in (w_gate, w_up, w_down, ws_gate, ws_up, ws_down))
        gm_w = (gm_ln_g[l], gm_ln_b[l], gm_w_s[l], gm_b_s[l])
        route = (w_o_b, ln1_g[l], ln1_b[l], w_router[l], router_bias[l], alpha)
        norm2 = (ln2_g[l], ln2_b[l], alpha)

        qp, kp, vp, k_pages, v_pages, gm = _proj_prompt(xp.reshape(n, t, dm), w_in_b, *gm_w, heads, hd, page)
        att = _moba_prompt(qp, kp, vp, hd)
        h, hp, _, plan, cnt = _outproj(xp, att.reshape(n * t, heads * hd), gm.reshape(n * t, -1), *route, 512)
        xp = _moe_sparse(h, hp, plan, cnt, *experts, *norm2)
        outs[0].append(k_pages)
        outs[1].append(v_pages)

        q_s, k_s, v_s, gm_s, vn_s = _proj_sample(xs, w_in_b, *gm_w, heads, hd)
        col4 = (ns, heads, hd, 1)
        idx = _sample_gate(q_s.reshape(col4), cache_kt, l, page_table)
        att_s = _sample_attention(q_s.reshape(col4), k_s.reshape(col4), v_s.reshape(col4),
                                  cache_kt, cache_vt, l, page_table, idx)
        h, hp, gates, _, _ = _outproj(xs, att_s.reshape(ns, heads * hd).astype(BF16), gm_s, *route, ns)
        xs = _moe_dense(h, hp, gates, *experts, *norm2)
        shp4 = (ns, heads, 1, hd)
        outs[2].append(k_s.reshape(shp4))
        outs[3].append(v_s.reshape(shp4))
        outs[4].append(vn_s.reshape(ns, 1, -1))
    return (xp.reshape(n, t, dm), xs.reshape(ns, 1, dm)) + tuple(jnp.stack(o) for o in outs)
```
